```python
import math
import jax, jax.numpy as jnp
from jax import lax
import numpy as np

D_MODEL = 4096
BATCH = 4
SEQ = 4096
DEPTH = 4

N_HEADS = 32
QK_NOPE = 128
QK_ROPE = 64
V_HEAD = 128
Q_LORA = 1024
KV_LORA = 512
MLA_WIDTH = N_HEADS * V_HEAD
QK_HEAD = QK_NOPE + QK_ROPE
ATTN_SCALE = 1.0 / math.sqrt(QK_HEAD)
ROPE_THETA = 10000.0
Q_BLOCK = 128
CONV_WIDTH = D_MODEL
CONV_K = 31
EPS = 1e-6
OFF_CQ = 0
OFF_CKV = OFF_CQ + Q_LORA
OFF_KR = OFF_CKV + KV_LORA
OFF_GMLA = OFF_KR + QK_ROPE
OFF_CONV = OFF_GMLA + MLA_WIDTH
OFF_GCONV = OFF_CONV + 2 * CONV_WIDTH
OFF_MA = OFF_GCONV + CONV_WIDTH
OFF_MC = OFF_MA + D_MODEL
IN_WIDTH = OFF_MC + D_MODEL

kernel_name = "mla_conformer_gated_hybrid"


def rms_norm(x, g):
    xf = x.astype(jnp.float32)
    y = xf * lax.rsqrt(jnp.mean(xf * xf, axis=-1, keepdims=True) + EPS)
    return (y * g.astype(jnp.float32)).astype(x.dtype)


def layer_norm(x, g, b):
    xf = x.astype(jnp.float32)
    mu = jnp.mean(xf, axis=-1, keepdims=True)
    xc = xf - mu
    var = jnp.mean(xc * xc, axis=-1, keepdims=True)
    y = xc * lax.rsqrt(var + EPS) * g.astype(jnp.float32) + b.astype(jnp.float32)
    return y.astype(x.dtype)


def rope_tables(positions, dtype):
    inv_freq = 1.0 / (ROPE_THETA ** (jnp.arange(0, QK_ROPE, 2, dtype=jnp.float32) / QK_ROPE))
    ang = positions.astype(jnp.float32)[..., None] * inv_freq
    return jnp.cos(ang).astype(dtype), jnp.sin(ang).astype(dtype)


def apply_rope(x, cos, sin):
    x1, x2 = jnp.split(x, 2, axis=-1)
    return jnp.concatenate([x1 * cos - x2 * sin, x2 * cos + x1 * sin], axis=-1)


def causal_mla_attention(q_nope, q_rope, k_nope, k_rope, v):
    b, s, h, _ = q_nope.shape
    nb = s // Q_BLOCK
    qn = q_nope.reshape(b, nb, Q_BLOCK, h, QK_NOPE).transpose(1, 0, 2, 3, 4)
    qr = q_rope.reshape(b, nb, Q_BLOCK, h, QK_ROPE).transpose(1, 0, 2, 3, 4)
    key_idx = jnp.arange(s)

    def one_block(args):
        qn_b, qr_b, blk = args
        sc = (jnp.einsum('bqhd,bkhd->bhqk', qn_b, k_nope)
              + jnp.einsum('bqhr,bkr->bhqk', qr_b, k_rope)).astype(jnp.float32) * ATTN_SCALE
        q_idx = blk * Q_BLOCK + jnp.arange(Q_BLOCK)
        mask = key_idx[None, :] <= q_idx[:, None]
        sc = jnp.where(mask[None, None], sc, -jnp.inf)
        p = jax.nn.softmax(sc, axis=-1).astype(v.dtype)
        return jnp.einsum('bhqk,bkhd->bqhd', p, v)

    out = lax.map(one_block, (qn, qr, jnp.arange(nb)))
    return out.transpose(1, 0, 2, 3, 4).reshape(b, s, h * V_HEAD)


def causal_depthwise_conv(u, w, bias):
    y = lax.conv_general_dilated(
        u, w[:, None, :].astype(u.dtype), window_strides=(1,),
        padding=[(CONV_K - 1, 0)], dimension_numbers=('NWC', 'WIO', 'NWC'),
        feature_group_count=u.shape[-1])
    return y + bias


def setup_inputs(seed: int = 0) -> dict:
    key = jax.random.key(seed)
    ks = jax.random.split(key, 20)
    f32 = jnp.float32

    def w(k, shape, fan_in):
        return jax.random.normal(k, shape, f32) * (fan_in ** -0.5)

    def gain(k, shape):
        return 1.0 + 0.01 * jax.random.normal(k, shape, f32)

    def bias(k, shape):
        return 0.01 * jax.random.normal(k, shape, f32)

    x = jax.random.normal(ks[0], (BATCH, SEQ, D_MODEL), f32)
    offs = jax.random.randint(ks[1], (BATCH, 1), 0, 4096, dtype=jnp.int32)
    positions = (jnp.arange(SEQ, dtype=jnp.int32)[None, :] + offs).astype(jnp.int32)
    return {
        "x": x,
        "positions": positions,
        "g_pre": gain(ks[2], (DEPTH, D_MODEL)),
        "w_in": w(ks[3], (DEPTH, D_MODEL, IN_WIDTH), D_MODEL),
        "g_q": gain(ks[4], (DEPTH, Q_LORA)),
        "w_q_up": w(ks[5], (DEPTH, Q_LORA, N_HEADS * QK_HEAD), Q_LORA),
        "g_kv": gain(ks[6], (DEPTH, KV_LORA)),
        "w_kv_up": w(ks[7], (DEPTH, KV_LORA, N_HEADS * (QK_NOPE + V_HEAD)), KV_LORA),
        "w_o_mla": w(ks[8], (DEPTH, MLA_WIDTH, D_MODEL), MLA_WIDTH),
        "w_dw": w(ks[9], (DEPTH, CONV_K, CONV_WIDTH), CONV_K),
        "b_dw": bias(ks[10], (DEPTH, CONV_WIDTH)),
        "g_cn": gain(ks[11], (DEPTH, CONV_WIDTH)),
        "b_cn": bias(ks[12], (DEPTH, CONV_WIDTH)),
        "w_pw_out": w(ks[13], (DEPTH, CONV_WIDTH, D_MODEL), CONV_WIDTH),
        "w_out": w(ks[14], (DEPTH, D_MODEL, D_MODEL), D_MODEL),
        "g_final": gain(ks[15], (D_MODEL,)),
    }


def reference(x, positions, g_pre, w_in, g_q, w_q_up, g_kv, w_kv_up, w_o_mla,
              w_dw, b_dw, g_cn, b_cn, w_pw_out, w_out, g_final):
    b, s, _ = x.shape
    cos, sin = rope_tables(positions, x.dtype)
    cos_h, sin_h = cos[:, :, None, :], sin[:, :, None, :]
    for l in range(DEPTH):
        h = rms_norm(x, g_pre[l])
        z = h @ w_in[l]
        c_q = z[..., OFF_CQ:OFF_CKV]
        c_kv = z[..., OFF_CKV:OFF_KR]
        k_rope = z[..., OFF_KR:OFF_GMLA]
        gate_mla = z[..., OFF_GMLA:OFF_CONV]
        conv_in = z[..., OFF_CONV:OFF_GCONV]
        gate_conv = z[..., OFF_GCONV:OFF_MA]
        merge_a = z[..., OFF_MA:OFF_MC]
        merge_c = z[..., OFF_MC:IN_WIDTH]

        q = (rms_norm(c_q, g_q[l]) @ w_q_up[l]).reshape(b, s, N_HEADS, QK_HEAD)
        q_nope, q_rope = q[..., :QK_NOPE], q[..., QK_NOPE:]
        kv = (rms_norm(c_kv, g_kv[l]) @ w_kv_up[l]).reshape(b, s, N_HEADS, QK_NOPE + V_HEAD)
        k_nope, v = kv[..., :QK_NOPE], kv[..., QK_NOPE:]
        q_rope = apply_rope(q_rope, cos_h, sin_h)
        k_rope = apply_rope(k_rope, cos, sin)
        attn = causal_mla_attention(q_nope, q_rope, k_nope, k_rope, v)
        y_a = (attn * jax.nn.silu(gate_mla)) @ w_o_mla[l]

        u_val, u_gate = conv_in[..., :CONV_WIDTH], conv_in[..., CONV_WIDTH:]
        u = u_val * jax.nn.sigmoid(u_gate)
        u = causal_depthwise_conv(u, w_dw[l], b_dw[l])
        u = jax.nn.silu(layer_norm(u, g_cn[l], b_cn[l]))
        y_c = (u * jax.nn.silu(gate_conv)) @ w_pw_out[l]

        y = jax.nn.sigmoid(merge_a) * y_a + jax.nn.sigmoid(merge_c) * y_c
        x = x + y @ w_out[l]
    return rms_norm(x, g_final)
```

```python
import functools
import math

import jax
import jax.numpy as jnp
from jax import lax
from jax.experimental import pallas as pl
from jax.experimental.pallas import tpu as pltpu

N_HEADS = 32
QK_NOPE = 128
QK_ROPE = 64
V_HEAD = 128
CONV_K = 31
EPS = 1e-6
ROPE_THETA = 10000.0

LANES = 128
HEAD_PAD = QK_NOPE + 2 * QK_ROPE
CONV_HALO = 32
CONV_TAPS_PAD = 32
VMEM_LIMIT_BYTES = 56 * 1024 * 1024

F32 = jnp.float32
BF16 = jnp.bfloat16

assert 2 * QK_ROPE == LANES and QK_NOPE == LANES and V_HEAD == LANES
assert CONV_HALO >= CONV_K - 1


def _params(*sem):
    return pltpu.CompilerParams(dimension_semantics=sem, vmem_limit_bytes=VMEM_LIMIT_BYTES)


def _dot(a, b):
    return jnp.dot(a, b, preferred_element_type=F32)


def _rope_mix(t):
    r = t + pltpu.roll(t, QK_ROPE, 1)
    lane = lax.broadcasted_iota(jnp.int32, r.shape, 1)
    return jnp.where(lane < QK_ROPE, r, 0.0)


def _rmsnorm_body(x_ref, g_ref, o_ref):
    x = x_ref[...]
    ms = jnp.mean(x * x, axis=-1, keepdims=True)
    o_ref[...] = (x * lax.rsqrt(ms + EPS) * g_ref[...]).astype(o_ref.dtype)


def _rmsnorm(x, g_row, out_dtype, tr):
    t, d = x.shape
    return pl.pallas_call(
        _rmsnorm_body,
        grid=(t // tr,),
        in_specs=[pl.BlockSpec((tr, d), lambda i: (i, 0)),
                  pl.BlockSpec((1, d), lambda i: (0, 0))],
        out_specs=pl.BlockSpec((tr, d), lambda i: (i, 0)),
        out_shape=jax.ShapeDtypeStruct((t, d), out_dtype),
        compiler_params=_params("parallel"),
        name="rmsnorm",
    )(x, g_row)


def _rope_table_body(pos_ref, inv_ref, o_ref):
    ang = pos_ref[...].astype(F32) * inv_ref[...]
    lane = lax.broadcasted_iota(jnp.int32, ang.shape, 1)
    c = jnp.cos(ang)
    s = jnp.sin(ang)
    o_ref[...] = jnp.where(lane < QK_ROPE, c, jnp.where(lane < QK_ROPE + QK_ROPE // 2, -s, s))


def _rope_table(pos_col, inv_row, tr):
    t = pos_col.shape[0]
    return pl.pallas_call(
        _rope_table_body,
        grid=(t // tr,),
        in_specs=[pl.BlockSpec((tr, 1), lambda i: (i, 0)),
                  pl.BlockSpec((1, LANES), lambda i: (0, 0))],
        out_specs=pl.BlockSpec((tr, LANES), lambda i: (i, 0)),
        out_shape=jax.ShapeDtypeStruct((t, LANES), F32),
        compiler_params=_params("parallel"),
        name="rope_table",
    )(pos_col, inv_row)


def _latent_body(h_ref, w_ref, gq_ref, gkv_ref, cs_ref, cq_ref, ckv_ref, kr_ref, *, q_lora, kv_lora):
    z = _dot(h_ref[...], w_ref[...])

    def rms(v, g):
        ms = jnp.mean(v * v, axis=-1, keepdims=True)
        return v * lax.rsqrt(ms + EPS) * g

    cq_ref[...] = rms(z[:, :q_lora], gq_ref[...]).astype(BF16)
    ckv_ref[...] = rms(z[:, q_lora:q_lora + kv_lora], gkv_ref[...]).astype(BF16)
    kr_ref[...] = _rope_mix(z[:, q_lora + kv_lora:] * cs_ref[...]).astype(BF16)


def _latent(h, w_lat, layer, gq_row, gkv_row, cs, tm):
    t, d = h.shape
    q_lora, kv_lora = gq_row.shape[1], gkv_row.shape[1]
    nl = w_lat.shape[2]
    return pl.pallas_call(
        functools.partial(_latent_body, q_lora=q_lora, kv_lora=kv_lora),
        grid=(t // tm,),
        in_specs=[pl.BlockSpec((tm, d), lambda i: (i, 0)),
                  pl.BlockSpec((None, d, nl), lambda i: (layer, 0, 0)),
                  pl.BlockSpec((1, q_lora), lambda i: (0, 0)),
                  pl.BlockSpec((1, kv_lora), lambda i: (0, 0)),
                  pl.BlockSpec((tm, LANES), lambda i: (i, 0))],
        out_specs=[pl.BlockSpec((tm, q_lora), lambda i: (i, 0)),
                   pl.BlockSpec((tm, kv_lora), lambda i: (i, 0)),
                   pl.BlockSpec((tm, LANES), lambda i: (i, 0))],
        out_shape=[jax.ShapeDtypeStruct((t, q_lora), BF16),
                   jax.ShapeDtypeStruct((t, kv_lora), BF16),
                   jax.ShapeDtypeStruct((t, LANES), BF16)],
        compiler_params=_params("parallel"),
        name="latent_proj",
    )(h, w_lat, gq_row, gkv_row, cs)


def _gates_body(h_ref, w_ref, o_ref, *, n_silu_tiles):
    z = _dot(h_ref[...], w_ref[...])
    s = jax.nn.sigmoid(z)
    is_silu = pl.program_id(1) < n_silu_tiles
    o_ref[...] = jnp.where(is_silu, z * s, s).astype(BF16)


def _gates(h, w_gates, layer, n_silu_cols, tm, tn):
    t, d = h.shape
    n = w_gates.shape[2]
    return pl.pallas_call(
        functools.partial(_gates_body, n_silu_tiles=n_silu_cols // tn),
        grid=(t // tm, n // tn),
        in_specs=[pl.BlockSpec((tm, d), lambda i, j: (i, 0)),
                  pl.BlockSpec((None, d, tn), lambda i, j: (layer, 0, j))],
        out_specs=pl.BlockSpec((tm, tn), lambda i, j: (i, j)),
        out_shape=jax.ShapeDtypeStruct((t, n), BF16),
        compiler_params=_params("parallel", "arbitrary"),
        name="gate_proj",
    )(h, w_gates)


def _glu_body(h_ref, wv_ref, wg_ref, o_ref):
    h = h_ref[...]
    u = _dot(h, wv_ref[...]) * jax.nn.sigmoid(_dot(h, wg_ref[...]))
    for c in range(o_ref.shape[0]):
        o_ref[c] = u[:, c * LANES:(c + 1) * LANES]


def _glu(h, w_conv, layer, seq, tm, tn):
    t, d = h.shape
    cw = w_conv.shape[2] // 2
    nsb = seq // tm
    nj = cw // tn
    return pl.pallas_call(
        _glu_body,
        grid=(t // tm, nj),
        in_specs=[pl.BlockSpec((tm, d), lambda i, j: (i, 0)),
                  pl.BlockSpec((None, d, tn), lambda i, j: (layer, 0, j)),
                  pl.BlockSpec((None, d, tn), lambda i, j: (layer, 0, j + nj))],
        out_specs=pl.BlockSpec((None, tn // LANES, tm, LANES), lambda i, j: (i // nsb, j, i % nsb, 0)),
        out_shape=jax.ShapeDtypeStruct((t // seq, cw // LANES, seq, LANES), F32),
        compiler_params=_params("parallel", "arbitrary"),
        name="glu_proj",
    )(h, w_conv, w_conv)


def _qup_body(a_ref, w_ref, cs_ref, o_ref, *, heads, scale):
    y = _dot(a_ref[...], w_ref[...])
    cs = cs_ref[...]
    for i in range(heads):
        c0 = i * HEAD_PAD
        o_ref[:, c0:c0 + QK_NOPE] = (y[:, c0:c0 + QK_NOPE] * scale).astype(BF16)
        o_ref[:, c0 + QK_NOPE:c0 + HEAD_PAD] = (_rope_mix(y[:, c0 + QK_NOPE:c0 + HEAD_PAD] * cs) * scale).astype(BF16)


def _qup(cq, w_q, layer, cs, scale, tm, heads):
    t, ql = cq.shape
    n = w_q.shape[2]
    tn = heads * HEAD_PAD
    return pl.pallas_call(
        functools.partial(_qup_body, heads=heads, scale=scale),
        grid=(t // tm, n // tn),
        in_specs=[pl.BlockSpec((tm, ql), lambda i, j: (i, 0)),
                  pl.BlockSpec((None, ql, tn), lambda i, j: (layer, 0, j)),
                  pl.BlockSpec((tm, LANES), lambda i, j: (i, 0))],
        out_specs=pl.BlockSpec((tm, tn), lambda i, j: (i, j)),
        out_shape=jax.ShapeDtypeStruct((t, n), BF16),
        compiler_params=_params("parallel", "arbitrary"),
        name="q_up",
    )(cq, w_q, cs)


def _kvup_body(a_ref, w_ref, kr_ref, k_ref, v_ref, *, heads):
    y = _dot(a_ref[...], w_ref[...])
    kr = kr_ref[...]
    for i in range(heads):
        c0 = i * HEAD_PAD
        k_ref[:, c0:c0 + QK_NOPE] = y[:, c0:c0 + QK_NOPE].astype(BF16)
        k_ref[:, c0 + QK_NOPE:c0 + HEAD_PAD] = kr
        v_ref[:, i * V_HEAD:(i + 1) * V_HEAD] = y[:, c0 + QK_NOPE:c0 + QK_NOPE + V_HEAD].astype(BF16)


def _kvup(ckv, w_kv, layer, kr, tm, heads):
    t, kvl = ckv.shape
    n_heads = w_kv.shape[2] // (QK_NOPE + V_HEAD)
    tn = heads * (QK_NOPE + V_HEAD)
    return pl.pallas_call(
        functools.partial(_kvup_body, heads=heads),
        grid=(t // tm, n_heads // heads),
        in_specs=[pl.BlockSpec((tm, kvl), lambda i, j: (i, 0)),
                  pl.BlockSpec((None, kvl, tn), lambda i, j: (layer, 0, j)),
                  pl.BlockSpec((tm, LANES), lambda i, j: (i, 0))],
        out_specs=[pl.BlockSpec((tm, heads * HEAD_PAD), lambda i, j: (i, j)),
                   pl.BlockSpec((tm, heads * V_HEAD), lambda i, j: (i, j))],
        out_shape=[jax.ShapeDtypeStruct((t, n_heads * HEAD_PAD), BF16),
                   jax.ShapeDtypeStruct((t, n_heads * V_HEAD), BF16)],
        compiler_params=_params("parallel", "arbitrary"),
        name="kv_up",
    )(ckv, w_kv, kr)


def _attn_body(q_ref, k_ref, v_ref, g_ref, o_ref, *, tq):
    qi = pl.program_id(2)
    q = q_ref[...]

    def step(k, v, carry, mask):
        m, l, acc = carry
        s = lax.dot_general(q, k, (((1,), (1,)), ((), ())), preferred_element_type=F32)
        if mask is not None:
            s = jnp.where(mask, s, -1e30)
        m_new = jnp.maximum(m, jnp.max(s, axis=-1, keepdims=True))
        alpha = jnp.exp2(m - m_new)
        p = jnp.exp2(s - m_new)
        l = alpha * l + jnp.sum(p, axis=-1, keepdims=True)
        acc = alpha * acc + _dot(p.astype(BF16), v)
        return m_new, l, acc

    def body(j, carry):
        off = pl.multiple_of(j * tq, tq)
        return step(k_ref[pl.ds(off, tq), :], v_ref[pl.ds(off, tq), :], carry, None)

    init = (jnp.full((tq, 1), -1e30, F32), jnp.zeros((tq, 1), F32), jnp.zeros((tq, V_HEAD), F32))
    carry = lax.fori_loop(0, qi, body, init)
    off = pl.multiple_of(qi * tq, tq)
    row = lax.broadcasted_iota(jnp.int32, (tq, tq), 0)
    col = lax.broadcasted_iota(jnp.int32, (tq, tq), 1)
    _, l, acc = step(k_ref[pl.ds(off, tq), :], v_ref[pl.ds(off, tq), :], carry, col <= row)
    o_ref[...] = (acc / l * g_ref[...].astype(F32)).astype(BF16)


def _attention(q, k, v, gates, n_heads, tq):
    b, s, _ = q.shape
    return pl.pallas_call(
        functools.partial(_attn_body, tq=tq),
        grid=(b, n_heads, s // tq),
        in_specs=[pl.BlockSpec((None, tq, HEAD_PAD), lambda bi, h, i: (bi, i, h)),
                  pl.BlockSpec((None, s, HEAD_PAD), lambda bi, h, i: (bi, 0, h)),
                  pl.BlockSpec((None, s, V_HEAD), lambda bi, h, i: (bi, 0, h)),
                  pl.BlockSpec((None, tq, V_HEAD), lambda bi, h, i: (bi, i, h))],
        out_specs=pl.BlockSpec((None, tq, V_HEAD), lambda bi, h, i: (bi, i, h)),
        out_shape=jax.ShapeDtypeStruct((b, s, n_heads * V_HEAD), BF16),
        compiler_params=_params("parallel", "parallel", "arbitrary"),
        name="mla_attention",
    )(q, k, v, gates)


def _conv_body(u_ref, halo_ref, w_ref, bdw_ref, gcn_ref, bcn_ref, gate_ref, o_ref, win_ref, y_ref,
               *, ts, rc, rf):
    nc = u_ref.shape[0]
    channels = nc * LANES
    i = pl.program_id(1)

    @pl.when(i == 0)
    def _():
        win_ref[:, 0:CONV_HALO, :] = jnp.zeros((nc, CONV_HALO, LANES), F32)

    @pl.when(i > 0)
    def _():
        win_ref[:, 0:CONV_HALO, :] = halo_ref[...]

    win_ref[:, CONV_HALO:CONV_HALO + ts, :] = u_ref[...]
    base = CONV_HALO - (CONV_K - 1)

    def conv_chunk(c, _):
        for r0 in range(0, ts, rc):
            acc = jnp.broadcast_to(bdw_ref[c], (rc, LANES))
            for k in range(CONV_K):
                acc = acc + win_ref[c, pl.ds(base + r0 + k, rc), :] * w_ref[c, k:k + 1, :]
            y_ref[c, pl.ds(r0, rc), :] = acc
        return 0

    lax.fori_loop(0, nc, conv_chunk, 0)

    def finish_rows(r, _):
        r0 = pl.multiple_of(r * rf, rf)
        ys = [y_ref[c, pl.ds(r0, rf), :] for c in range(nc)]
        tot = ys[0]
        for y in ys[1:]:
            tot = tot + y
        mu = jnp.sum(tot, axis=-1, keepdims=True) / channels
        ds = [y - mu for y in ys]
        sq = ds[0] * ds[0]
        for d in ds[1:]:
            sq = sq + d * d
        rstd = lax.rsqrt(jnp.sum(sq, axis=-1, keepdims=True) / channels + EPS)
        for c in range(nc):
            z = ds[c] * rstd * gcn_ref[c] + bcn_ref[c]
            z = z * jax.nn.sigmoid(z)
            g = gate_ref[pl.ds(r0, rf), c * LANES:(c + 1) * LANES].astype(F32)
            o_ref[pl.ds(r0, rf), c * LANES:(c + 1) * LANES] = (z * g).astype(BF16)
        return 0

    lax.fori_loop(0, ts // rf, finish_rows, 0)


def _conv_module(u, w_slab, bdw_slab, gcn_slab, bcn_slab, gates, gate_col0, layer, ts):
    b, nc, s, _ = u.shape
    c = nc * LANES
    hb = ts // CONV_HALO
    gblk = gate_col0 // c
    return pl.pallas_call(
        functools.partial(_conv_body, ts=ts, rc=min(64, ts), rf=16),
        grid=(b, s // ts),
        in_specs=[pl.BlockSpec((None, nc, ts, LANES), lambda bi, i: (bi, 0, i, 0)),
                  pl.BlockSpec((None, nc, CONV_HALO, LANES),
                               lambda bi, i: (bi, 0, jnp.maximum(i * hb - 1, 0), 0)),
                  pl.BlockSpec((None, nc, CONV_TAPS_PAD, LANES), lambda bi, i: (layer, 0, 0, 0)),
                  pl.BlockSpec((None, nc, 1, LANES), lambda bi, i: (layer, 0, 0, 0)),
                  pl.BlockSpec((None, nc, 1, LANES), lambda bi, i: (layer, 0, 0, 0)),
                  pl.BlockSpec((None, nc, 1, LANES), lambda bi, i: (layer, 0, 0, 0)),
                  pl.BlockSpec((None, ts, c), lambda bi, i: (bi, i, gblk))],
        out_specs=pl.BlockSpec((None, ts, c), lambda bi, i: (bi, i, 0)),
        out_shape=jax.ShapeDtypeStruct((b, s, c), BF16),
        scratch_shapes=[pltpu.VMEM((nc, CONV_HALO + ts, LANES), F32),
                        pltpu.VMEM((nc, ts, LANES), F32)],
        compiler_params=_params("parallel", "arbitrary"),
        name="conv_module",
    )(u, u, w_slab, bdw_slab, gcn_slab, bcn_slab, gates)


def _merge_body(a_ref, c_ref, wo_ref, wpw_ref, sa_ref, sc_ref, o_ref):
    ya = _dot(a_ref[...], wo_ref[...])
    yc = _dot(c_ref[...], wpw_ref[...])
    o_ref[...] = (sa_ref[...].astype(F32) * ya + sc_ref[...].astype(F32) * yc).astype(BF16)


def _merge(a, cc, w_o, w_pw, layer, gates, col_a, col_c, tm, tn):
    t, mw = a.shape
    cw = cc.shape[1]
    d = w_o.shape[2]
    ja, jc = col_a // tn, col_c // tn
    return pl.pallas_call(
        _merge_body,
        grid=(t // tm, d // tn),
        in_specs=[pl.BlockSpec((tm, mw), lambda i, j: (i, 0)),
                  pl.BlockSpec((tm, cw), lambda i, j: (i, 0)),
                  pl.BlockSpec((None, mw, tn), lambda i, j: (layer, 0, j)),
                  pl.BlockSpec((None, cw, tn), lambda i, j: (layer, 0, j)),
                  pl.BlockSpec((tm, tn), lambda i, j: (i, ja + j)),
                  pl.BlockSpec((tm, tn), lambda i, j: (i, jc + j))],
        out_specs=pl.BlockSpec((tm, tn), lambda i, j: (i, j)),
        out_shape=jax.ShapeDtypeStruct((t, d), BF16),
        compiler_params=_params("parallel", "arbitrary"),
        name="merge_proj",
    )(a, cc, w_o, w_pw, gates, gates)


def _outproj_body(y_ref, w_ref, x_ref, o_ref):
    o_ref[...] = x_ref[...] + _dot(y_ref[...], w_ref[...])


def _outproj(y, w_out, layer, x, tm, tn):
    t, d = y.shape
    n = w_out.shape[2]
    return pl.pallas_call(
        _outproj_body,
        grid=(t // tm, n // tn),
        in_specs=[pl.BlockSpec((tm, d), lambda i, j: (i, 0)),
                  pl.BlockSpec((None, d, tn), lambda i, j: (layer, 0, j)),
                  pl.BlockSpec((tm, tn), lambda i, j: (i, j))],
        out_specs=pl.BlockSpec((tm, tn), lambda i, j: (i, j)),
        out_shape=jax.ShapeDtypeStruct((t, n), F32),
        compiler_params=_params("parallel", "arbitrary"),
        name="out_proj",
    )(y, w_out, x)


def _slab(p):
    l, r, c = p.shape
    return p.reshape(l, r, c // LANES, LANES).transpose(0, 2, 1, 3)


def _forward(x, positions, g_pre, w_in, g_q, w_q_up, g_kv, w_kv_up, w_o_mla, w_dw, b_dw, g_cn, b_cn,
             w_pw_out, w_out, g_final, *, n_heads):
    b, s, d = x.shape
    depth = g_pre.shape[0]
    t = b * s
    q_lora, kv_lora = g_q.shape[1], g_kv.shape[1]
    mw = n_heads * V_HEAD
    cw = w_dw.shape[2]
    off_kr = q_lora + kv_lora
    off_gmla = off_kr + QK_ROPE
    off_conv = off_gmla + mw
    off_gconv = off_conv + 2 * cw
    assert w_in.shape[2] == off_gconv + cw + 2 * d and cw == d

    tm = min(1024, s)
    tn = min(512, d)
    tq = min(512, s)
    ts = min(256, s)
    heads_per_tile = min(8, n_heads)

    swap = jnp.concatenate([jnp.arange(QK_ROPE // 2, QK_ROPE), jnp.arange(0, QK_ROPE // 2)])
    w_kr = w_in[:, :, off_kr:off_gmla]
    w_lat = jnp.concatenate([w_in[:, :, :off_gmla], w_kr[:, :, swap]], axis=2).astype(BF16)
    w_gates = jnp.concatenate([w_in[:, :, off_gmla:off_conv], w_in[:, :, off_gconv:]], axis=2).astype(BF16)
    w_conv = w_in[:, :, off_conv:off_gconv].astype(BF16)
    wq = w_q_up.reshape(depth, q_lora, n_heads, QK_NOPE + QK_ROPE)
    w_q = jnp.concatenate([wq, wq[..., QK_NOPE:][..., swap]], axis=3).reshape(depth, q_lora, n_heads * HEAD_PAD)
    w_q = w_q.astype(BF16)
    w_kv = w_kv_up.astype(BF16)
    w_o = w_o_mla.astype(BF16)
    w_pw = w_pw_out.astype(BF16)
    w_o2 = w_out.astype(BF16)
    w_slab = _slab(jnp.pad(w_dw, ((0, 0), (0, CONV_TAPS_PAD - CONV_K), (0, 0))))
    bdw_slab, gcn_slab, bcn_slab = _slab(b_dw[:, None]), _slab(g_cn[:, None]), _slab(b_cn[:, None])

    inv_freq = 1.0 / (ROPE_THETA ** (jnp.arange(0, QK_ROPE, 2, dtype=F32) / QK_ROPE))
    cs = _rope_table(positions.reshape(t, 1), jnp.tile(inv_freq, 4)[None], min(1024, t))

    q_scale = (1.0 / math.sqrt(QK_NOPE + QK_ROPE)) * math.log2(math.e)
    x2 = x.reshape(t, d)
    h = _rmsnorm(x2, g_pre[0][None], BF16, min(256, t))
    for l in range(depth):
        cq, ckv, kr = _latent(h, w_lat, l, g_q[l][None], g_kv[l][None], cs, min(512, s))
        gates = _gates(h, w_gates, l, mw + cw, tm, tn)
        u = _glu(h, w_conv, l, s, tm, tn)
        q = _qup(cq, w_q, l, cs, q_scale, tm, heads_per_tile)
        k, v = _kvup(ckv, w_kv, l, kr, tm, heads_per_tile)
        gates3 = gates.reshape(b, s, -1)
        attn = _attention(q.reshape(b, s, -1), k.reshape(b, s, -1), v.reshape(b, s, -1), gates3, n_heads, tq)
        cc = _conv_module(u, w_slab, bdw_slab, gcn_slab, bcn_slab, gates3, mw, l, ts)
        y = _merge(attn.reshape(t, mw), cc.reshape(t, cw), w_o, w_pw, l, gates, mw + cw, mw + cw + d, tm, min(256, d))
        x2 = _outproj(y, w_o2, l, x2, tm, tn)
        if l + 1 < depth:
            h = _rmsnorm(x2, g_pre[l + 1][None], BF16, min(256, t))
    return _rmsnorm(x2, g_final[None], F32, min(256, t)).reshape(b, s, d)


def kernel(x, positions, g_pre, w_in, g_q, w_q_up, g_kv, w_kv_up, w_o_mla, w_dw, b_dw, g_cn, b_cn, w_pw_out, w_out, g_final):
    return _forward(x, positions, g_pre, w_in, g_q, w_q_up, g_kv, w_kv_up, w_o_mla, w_dw, b_dw, g_cn, b_cn,
                    w_pw_out, w_out, g_final, n_heads=N_HEADS)
```

```python
import functools
import math

import jax
import jax.numpy as jnp
from jax import lax
from jax.experimental import pallas as pl
from jax.experimental.pallas import tpu as pltpu

N_HEADS = 32
QK_NOPE = 128
QK_ROPE = 64
V_HEAD = 128
CONV_K = 31
EPS = 1e-6
ROPE_THETA = 10000.0

LANES = 128
HEAD_PAD = QK_NOPE + 2 * QK_ROPE
CONV_HALO = 32
CONV_TAPS_PAD = 32
ATTN_LOOKAHEAD = 3
VMEM_LIMIT_BYTES = 56 * 1024 * 1024

F32 = jnp.float32
BF16 = jnp.bfloat16

assert 2 * QK_ROPE == LANES and QK_NOPE == LANES and V_HEAD == LANES
assert CONV_HALO >= CONV_K - 1


def _params(*sem):
    return pltpu.CompilerParams(dimension_semantics=sem, vmem_limit_bytes=VMEM_LIMIT_BYTES)


def _dot(a, b):
    return jnp.dot(a, b, preferred_element_type=F32)


def _rope_mix(t):
    r = t + pltpu.roll(t, QK_ROPE, 1)
    lane = lax.broadcasted_iota(jnp.int32, r.shape, 1)
    return jnp.where(lane < QK_ROPE, r, 0.0)


def _rmsnorm_body(x_ref, g_ref, o_ref):
    x = x_ref[...]
    ms = jnp.mean(x * x, axis=-1, keepdims=True)
    o_ref[...] = (x * lax.rsqrt(ms + EPS) * g_ref[...]).astype(o_ref.dtype)


def _rmsnorm(x, g_row, out_dtype, tr):
    t, d = x.shape
    return pl.pallas_call(
        _rmsnorm_body,
        grid=(t // tr,),
        in_specs=[pl.BlockSpec((tr, d), lambda i: (i, 0)),
                  pl.BlockSpec((1, d), lambda i: (0, 0))],
        out_specs=pl.BlockSpec((tr, d), lambda i: (i, 0)),
        out_shape=jax.ShapeDtypeStruct((t, d), out_dtype),
        compiler_params=_params("parallel"),
        name="rmsnorm",
    )(x, g_row)


def _rope_table_body(pos_ref, inv_ref, o_ref):
    ang = pos_ref[...].astype(F32) * inv_ref[...]
    lane = lax.broadcasted_iota(jnp.int32, ang.shape, 1)
    c = jnp.cos(ang)
    s = jnp.sin(ang)
    o_ref[...] = jnp.where(lane < QK_ROPE, c, jnp.where(lane < QK_ROPE + QK_ROPE // 2, -s, s))


def _rope_table(pos_col, inv_row, tr):
    t = pos_col.shape[0]
    return pl.pallas_call(
        _rope_table_body,
        grid=(t // tr,),
        in_specs=[pl.BlockSpec((tr, 1), lambda i: (i, 0)),
                  pl.BlockSpec((1, LANES), lambda i: (0, 0))],
        out_specs=pl.BlockSpec((tr, LANES), lambda i: (i, 0)),
        out_shape=jax.ShapeDtypeStruct((t, LANES), F32),
        compiler_params=_params("parallel"),
        name="rope_table",
    )(pos_col, inv_row)


def _latent_body(h_ref, w_ref, gq_ref, gkv_ref, cs_ref, cq_ref, ckv_ref, kr_ref, *, q_lora, kv_lora):
    z = _dot(h_ref[...], w_ref[...])

    def rms(v, g):
        ms = jnp.mean(v * v, axis=-1, keepdims=True)
        return v * lax.rsqrt(ms + EPS) * g

    cq_ref[...] = rms(z[:, :q_lora], gq_ref[...]).astype(BF16)
    ckv_ref[...] = rms(z[:, q_lora:q_lora + kv_lora], gkv_ref[...]).astype(BF16)
    kr_ref[...] = _rope_mix(z[:, q_lora + kv_lora:] * cs_ref[...]).astype(BF16)


def _latent(h, w_lat, layer, gq_row, gkv_row, cs, tm):
    t, d = h.shape
    q_lora, kv_lora = gq_row.shape[1], gkv_row.shape[1]
    nl = w_lat.shape[2]
    return pl.pallas_call(
        functools.partial(_latent_body, q_lora=q_lora, kv_lora=kv_lora),
        grid=(t // tm,),
        in_specs=[pl.BlockSpec((tm, d), lambda i: (i, 0)),
                  pl.BlockSpec((None, d, nl), lambda i: (layer, 0, 0)),
                  pl.BlockSpec((1, q_lora), lambda i: (0, 0)),
                  pl.BlockSpec((1, kv_lora), lambda i: (0, 0)),
                  pl.BlockSpec((tm, LANES), lambda i: (i, 0))],
        out_specs=[pl.BlockSpec((tm, q_lora), lambda i: (i, 0)),
                   pl.BlockSpec((tm, kv_lora), lambda i: (i, 0)),
                   pl.BlockSpec((tm, LANES), lambda i: (i, 0))],
        out_shape=[jax.ShapeDtypeStruct((t, q_lora), BF16),
                   jax.ShapeDtypeStruct((t, kv_lora), BF16),
                   jax.ShapeDtypeStruct((t, LANES), BF16)],
        compiler_params=_params("parallel"),
        name="latent_proj",
    )(h, w_lat, gq_row, gkv_row, cs)


def _gates_body(h_ref, w_ref, o_ref, *, n_silu_tiles):
    z = _dot(h_ref[...], w_ref[...])
    s = jax.nn.sigmoid(z)
    is_silu = pl.program_id(1) < n_silu_tiles
    o_ref[...] = jnp.where(is_silu, z * s, s).astype(BF16)


def _gates(h, w_gates, layer, n_silu_cols, tm, tn):
    t, d = h.shape
    n = w_gates.shape[2]
    return pl.pallas_call(
        functools.partial(_gates_body, n_silu_tiles=n_silu_cols // tn),
        grid=(t // tm, n // tn),
        in_specs=[pl.BlockSpec((tm, d), lambda i, j: (i, 0)),
                  pl.BlockSpec((None, d, tn), lambda i, j: (layer, 0, j))],
        out_specs=pl.BlockSpec((tm, tn), lambda i, j: (i, j)),
        out_shape=jax.ShapeDtypeStruct((t, n), BF16),
        compiler_params=_params("parallel", "arbitrary"),
        name="gate_proj",
    )(h, w_gates)


def _glu_body(h_ref, wv_ref, wg_ref, o_ref):
    h = h_ref[...]
    u = _dot(h, wv_ref[...]) * jax.nn.sigmoid(_dot(h, wg_ref[...]))
    for c in range(o_ref.shape[0]):
        o_ref[c] = u[:, c * LANES:(c + 1) * LANES]


def _glu(h, w_conv, layer, seq, tm, tn):
    t, d = h.shape
    cw = w_conv.shape[2] // 2
    nsb = seq // tm
    nj = cw // tn
    return pl.pallas_call(
        _glu_body,
        grid=(t // tm, nj),
        in_specs=[pl.BlockSpec((tm, d), lambda i, j: (i, 0)),
                  pl.BlockSpec((None, d, tn), lambda i, j: (layer, 0, j)),
                  pl.BlockSpec((None, d, tn), lambda i, j: (layer, 0, j + nj))],
        out_specs=pl.BlockSpec((None, tn // LANES, tm, LANES), lambda i, j: (i // nsb, j, i % nsb, 0)),
        out_shape=jax.ShapeDtypeStruct((t // seq, cw // LANES, seq, LANES), F32),
        compiler_params=_params("parallel", "arbitrary"),
        name="glu_proj",
    )(h, w_conv, w_conv)


def _qup_body(a_ref, w_ref, cs_ref, o_ref, *, heads, scale):
    y = _dot(a_ref[...], w_ref[...])
    cs = cs_ref[...]
    for i in range(heads):
        c0 = i * HEAD_PAD
        o_ref[:, c0:c0 + QK_NOPE] = (y[:, c0:c0 + QK_NOPE] * scale).astype(BF16)
        o_ref[:, c0 + QK_NOPE:c0 + HEAD_PAD] = (_rope_mix(y[:, c0 + QK_NOPE:c0 + HEAD_PAD] * cs) * scale).astype(BF16)


def _qup(cq, w_q, layer, cs, scale, tm, heads):
    t, ql = cq.shape
    n = w_q.shape[2]
    tn = heads * HEAD_PAD
    return pl.pallas_call(
        functools.partial(_qup_body, heads=heads, scale=scale),
        grid=(t // tm, n // tn),
        in_specs=[pl.BlockSpec((tm, ql), lambda i, j: (i, 0)),
                  pl.BlockSpec((None, ql, tn), lambda i, j: (layer, 0, j)),
                  pl.BlockSpec((tm, LANES), lambda i, j: (i, 0))],
        out_specs=pl.BlockSpec((tm, tn), lambda i, j: (i, j)),
        out_shape=jax.ShapeDtypeStruct((t, n), BF16),
        compiler_params=_params("parallel", "arbitrary"),
        name="q_up",
    )(cq, w_q, cs)


def _kvup_body(a_ref, w_ref, kr_ref, k_ref, v_ref, *, heads):
    y = _dot(a_ref[...], w_ref[...])
    kr = kr_ref[...]
    for i in range(heads):
        c0 = i * HEAD_PAD
        k_ref[:, c0:c0 + QK_NOPE] = y[:, c0:c0 + QK_NOPE].astype(BF16)
        k_ref[:, c0 + QK_NOPE:c0 + HEAD_PAD] = kr
        v_ref[:, i * V_HEAD:(i + 1) * V_HEAD] = y[:, c0 + QK_NOPE:c0 + QK_NOPE + V_HEAD].astype(BF16)


def _kvup(ckv, w_kv, layer, kr, tm, heads):
    t, kvl = ckv.shape
    n_heads = w_kv.shape[2] // (QK_NOPE + V_HEAD)
    tn = heads * (QK_NOPE + V_HEAD)
    return pl.pallas_call(
        functools.partial(_kvup_body, heads=heads),
        grid=(t // tm, n_heads // heads),
        in_specs=[pl.BlockSpec((tm, kvl), lambda i, j: (i, 0)),
                  pl.BlockSpec((None, kvl, tn), lambda i, j: (layer, 0, j)),
                  pl.BlockSpec((tm, LANES), lambda i, j: (i, 0))],
        out_specs=[pl.BlockSpec((tm, heads * HEAD_PAD), lambda i, j: (i, j)),
                   pl.BlockSpec((tm, heads * V_HEAD), lambda i, j: (i, j))],
        out_shape=[jax.ShapeDtypeStruct((t, n_heads * HEAD_PAD), BF16),
                   jax.ShapeDtypeStruct((t, n_heads * V_HEAD), BF16)],
        compiler_params=_params("parallel", "arbitrary"),
        name="kv_up",
    )(ckv, w_kv, kr)


def _attn_body(q_ref, k_ref, v_ref, g_ref, o_ref, vt_ref, qt_ref, *, tq, tk):
    qi = pl.program_id(2)
    n_kv = v_ref.shape[0] // tk
    cpq = tq // tk

    @pl.when(qi == 0)
    def _():
        for c in range(n_kv):
            vt_ref[c] = v_ref[c * tk:(c + 1) * tk, :].T

    qt_ref[...] = q_ref[...].T

    def scores(c, q0):
        return _dot(k_ref[c * tk:(c + 1) * tk, :], qt_ref[:, q0:])

    def update(c, st, state, q0, masked):
        m, l, acc = (x[:, q0:] for x in state)
        if masked:
            key = lax.broadcasted_iota(jnp.int32, st.shape, 0)
            qry = lax.broadcasted_iota(jnp.int32, st.shape, 1)
            st = jnp.where(key <= qry, st, -1e30)
        m_new = jnp.maximum(m, jnp.max(st, axis=0, keepdims=True))
        alpha = jnp.exp2(m - m_new)
        pt = jnp.exp2(st - m_new)
        l = alpha * l + jnp.sum(pt, axis=0, keepdims=True)
        acc = alpha * acc + _dot(vt_ref[c], pt.astype(BF16))
        if q0 == 0:
            return m_new, l, acc
        return tuple(jnp.concatenate([old[:, :q0], new], axis=1) for old, new in zip(state, (m_new, l, acc)))

    def q_tile_program(t):
        n_chunks = (t + 1) * cpq
        q0_of = lambda c: max(c - t * cpq, 0) * tk
        sts = {c: scores(c, q0_of(c)) for c in range(min(ATTN_LOOKAHEAD, n_chunks))}
        state = (jnp.full((1, tq), -1e30, F32), jnp.zeros((1, tq), F32), jnp.zeros((V_HEAD, tq), F32))
        for c in range(n_chunks):
            state = update(c, sts.pop(c), state, q0_of(c), c >= t * cpq)
            if c + ATTN_LOOKAHEAD < n_chunks:
                sts[c + ATTN_LOOKAHEAD] = scores(c + ATTN_LOOKAHEAD, q0_of(c + ATTN_LOOKAHEAD))
        _, l, acc = state
        out = (acc * (1.0 / l)).T
        o_ref[...] = (out * g_ref[...].astype(F32)).astype(BF16)

    for t in range(n_kv // cpq):
        pl.when(qi == t)(functools.partial(q_tile_program, t))


def _attention(q, k, v, gates, n_heads, tq, tk):
    b, s, _ = q.shape
    return pl.pallas_call(
        functools.partial(_attn_body, tq=tq, tk=tk),
        grid=(b, n_heads, s // tq),
        scratch_shapes=[pltpu.VMEM((s // tk, V_HEAD, tk), BF16),
                        pltpu.VMEM((HEAD_PAD, tq), BF16)],
        in_specs=[pl.BlockSpec((None, tq, HEAD_PAD), lambda bi, h, i: (bi, i, h)),
                  pl.BlockSpec((None, s, HEAD_PAD), lambda bi, h, i: (bi, 0, h)),
                  pl.BlockSpec((None, s, V_HEAD), lambda bi, h, i: (bi, 0, h)),
                  pl.BlockSpec((None, tq, V_HEAD), lambda bi, h, i: (bi, i, h))],
        out_specs=pl.BlockSpec((None, tq, V_HEAD), lambda bi, h, i: (bi, i, h)),
        out_shape=jax.ShapeDtypeStruct((b, s, n_heads * V_HEAD), BF16),
        compiler_params=_params("parallel", "parallel", "arbitrary"),
        name="mla_attention",
    )(q, k, v, gates)


def _conv_body(u_ref, halo_ref, w_ref, bdw_ref, gcn_ref, bcn_ref, gate_ref, o_ref, win_ref, y_ref,
               *, ts, rc, rf):
    nc = u_ref.shape[0]
    channels = nc * LANES
    i = pl.program_id(1)

    @pl.when(i == 0)
    def _():
        win_ref[:, 0:CONV_HALO, :] = jnp.zeros((nc, CONV_HALO, LANES), F32)

    @pl.when(i > 0)
    def _():
        win_ref[:, 0:CONV_HALO, :] = halo_ref[...]

    win_ref[:, CONV_HALO:CONV_HALO + ts, :] = u_ref[...]
    base = CONV_HALO - (CONV_K - 1)

    def conv_chunk(c, _):
        for r0 in range(0, ts, rc):
            acc = jnp.broadcast_to(bdw_ref[c], (rc, LANES))
            for k in range(CONV_K):
                acc = acc + win_ref[c, pl.ds(base + r0 + k, rc), :] * w_ref[c, k:k + 1, :]
            y_ref[c, pl.ds(r0, rc), :] = acc
        return 0

    lax.fori_loop(0, nc, conv_chunk, 0)

    inv_c = 1.0 / channels
    for r0 in range(0, ts, rf):
        tot = y_ref[0, r0:r0 + rf, :]
        for c in range(1, nc):
            tot = tot + y_ref[c, r0:r0 + rf, :]
        mu = jnp.broadcast_to(jnp.sum(tot, axis=-1, keepdims=True) * inv_c, (rf, LANES))
        sq = None
        for c in range(nc):
            d = y_ref[c, r0:r0 + rf, :] - mu
            sq = d * d if sq is None else sq + d * d
        rstd = jnp.broadcast_to(lax.rsqrt(jnp.sum(sq, axis=-1, keepdims=True) * inv_c + EPS), (rf, LANES))
        for c in range(nc):
            z = (y_ref[c, r0:r0 + rf, :] - mu) * rstd * gcn_ref[c] + bcn_ref[c]
            z = z * jax.nn.sigmoid(z)
            g = gate_ref[r0:r0 + rf, c * LANES:(c + 1) * LANES].astype(F32)
            o_ref[r0:r0 + rf, c * LANES:(c + 1) * LANES] = (z * g).astype(BF16)


def _conv_module(u, w_slab, bdw_slab, gcn_slab, bcn_slab, gates, gate_col0, layer, ts):
    b, nc, s, _ = u.shape
    c = nc * LANES
    hb = ts // CONV_HALO
    gblk = gate_col0 // c
    return pl.pallas_call(
        functools.partial(_conv_body, ts=ts, rc=min(64, ts), rf=min(64, ts)),
        grid=(b, s // ts),
        in_specs=[pl.BlockSpec((None, nc, ts, LANES), lambda bi, i: (bi, 0, i, 0)),
                  pl.BlockSpec((None, nc, CONV_HALO, LANES),
                               lambda bi, i: (bi, 0, jnp.maximum(i * hb - 1, 0), 0)),
                  pl.BlockSpec((None, nc, CONV_TAPS_PAD, LANES), lambda bi, i: (layer, 0, 0, 0)),
                  pl.BlockSpec((None, nc, 1, LANES), lambda bi, i: (layer, 0, 0, 0)),
                  pl.BlockSpec((None, nc, 1, LANES), lambda bi, i: (layer, 0, 0, 0)),
                  pl.BlockSpec((None, nc, 1, LANES), lambda bi, i: (layer, 0, 0, 0)),
                  pl.BlockSpec((None, ts, c), lambda bi, i: (bi, i, gblk))],
        out_specs=pl.BlockSpec((None, ts, c), lambda bi, i: (bi, i, 0)),
        out_shape=jax.ShapeDtypeStruct((b, s, c), BF16),
        scratch_shapes=[pltpu.VMEM((nc, CONV_HALO + ts, LANES), F32),
                        pltpu.VMEM((nc, ts, LANES), F32)],
        compiler_params=_params("parallel", "arbitrary"),
        name="conv_module",
    )(u, u, w_slab, bdw_slab, gcn_slab, bcn_slab, gates)


def _merge_body(a_ref, c_ref, wo_ref, wpw_ref, sa_ref, sc_ref, o_ref):
    ya = _dot(a_ref[...], wo_ref[...])
    yc = _dot(c_ref[...], wpw_ref[...])
    o_ref[...] = (sa_ref[...].astype(F32) * ya + sc_ref[...].astype(F32) * yc).astype(BF16)


def _merge(a, cc, w_o, w_pw, layer, gates, col_a, col_c, tm, tn):
    t, mw = a.shape
    cw = cc.shape[1]
    d = w_o.shape[2]
    ja, jc = col_a // tn, col_c // tn
    return pl.pallas_call(
        _merge_body,
        grid=(t // tm, d // tn),
        in_specs=[pl.BlockSpec((tm, mw), lambda i, j: (i, 0)),
                  pl.BlockSpec((tm, cw), lambda i, j: (i, 0)),
                  pl.BlockSpec((None, mw, tn), lambda i, j: (layer, 0, j)),
                  pl.BlockSpec((None, cw, tn), lambda i, j: (layer, 0, j)),
                  pl.BlockSpec((tm, tn), lambda i, j: (i, ja + j)),
                  pl.BlockSpec((tm, tn), lambda i, j: (i, jc + j))],
        out_specs=pl.BlockSpec((tm, tn), lambda i, j: (i, j)),
        out_shape=jax.ShapeDtypeStruct((t, d), BF16),
        compiler_params=_params("parallel", "arbitrary"),
        name="merge_proj",
    )(a, cc, w_o, w_pw, gates, gates)


def _outproj_body(y_ref, w_ref, x_ref, o_ref):
    o_ref[...] = x_ref[...] + _dot(y_ref[...], w_ref[...])


def _outproj(y, w_out, layer, x, tm, tn):
    t, d = y.shape
    n = w_out.shape[2]
    return pl.pallas_call(
        _outproj_body,
        grid=(t // tm, n // tn),
        in_specs=[pl.BlockSpec((tm, d), lambda i, j: (i, 0)),
                  pl.BlockSpec((None, d, tn), lambda i, j: (layer, 0, j)),
                  pl.BlockSpec((tm, tn), lambda i, j: (i, j))],
        out_specs=pl.BlockSpec((tm, tn), lambda i, j: (i, j)),
        out_shape=jax.ShapeDtypeStruct((t, n), F32),
        compiler_params=_params("parallel", "arbitrary"),
        name="out_proj",
    )(y, w_out, x)


def _slab(p):
    l, r, c = p.shape
    return p.reshape(l, r, c // LANES, LANES).transpose(0, 2, 1, 3)


def _forward(x, positions, g_pre, w_in, g_q, w_q_up, g_kv, w_kv_up, w_o_mla, w_dw, b_dw, g_cn, b_cn,
             w_pw_out, w_out, g_final, *, n_heads):
    b, s, d = x.shape
    depth = g_pre.shape[0]
    t = b * s
    q_lora, kv_lora = g_q.shape[1], g_kv.shape[1]
    mw = n_heads * V_HEAD
    cw = w_dw.shape[2]
    off_kr = q_lora + kv_lora
    off_gmla = off_kr + QK_ROPE
    off_conv = off_gmla + mw
    off_gconv = off_conv + 2 * cw
    assert w_in.shape[2] == off_gconv + cw + 2 * d and cw == d

    tm = min(1024, s)
    tn = min(512, d)
    tn_gate = min(1024, d)
    tq = min(512, s)
    tk = min(256, s)
    ts = min(256, s)
    heads_per_tile = min(8, n_heads)

    swap = jnp.concatenate([jnp.arange(QK_ROPE // 2, QK_ROPE), jnp.arange(0, QK_ROPE // 2)])
    w_kr = w_in[:, :, off_kr:off_gmla]
    w_lat = jnp.concatenate([w_in[:, :, :off_gmla], w_kr[:, :, swap]], axis=2).astype(BF16)
    w_gmla = w_in[:, :, off_gmla:off_conv].astype(BF16)
    w_gmix = w_in[:, :, off_gconv:].astype(BF16)
    w_conv = w_in[:, :, off_conv:off_gconv].astype(BF16)
    wq = w_q_up.reshape(depth, q_lora, n_heads, QK_NOPE + QK_ROPE)
    w_q = jnp.concatenate([wq, wq[..., QK_NOPE:][..., swap]], axis=3).reshape(depth, q_lora, n_heads * HEAD_PAD)
    w_q = w_q.astype(BF16)
    w_kv = w_kv_up.astype(BF16)
    w_o = w_o_mla.astype(BF16)
    w_pw = w_pw_out.astype(BF16)
    w_o2 = w_out.astype(BF16)
    w_slab = _slab(jnp.pad(w_dw, ((0, 0), (0, CONV_TAPS_PAD - CONV_K), (0, 0))))
    bdw_slab, gcn_slab, bcn_slab = _slab(b_dw[:, None]), _slab(g_cn[:, None]), _slab(b_cn[:, None])

    inv_freq = 1.0 / (ROPE_THETA ** (jnp.arange(0, QK_ROPE, 2, dtype=F32) / QK_ROPE))
    cs = _rope_table(positions.reshape(t, 1), jnp.tile(inv_freq, 4)[None], min(1024, t))

    q_scale = (1.0 / math.sqrt(QK_NOPE + QK_ROPE)) * math.log2(math.e)
    x2 = x.reshape(t, d)
    h = _rmsnorm(x2, g_pre[0][None], BF16, min(256, t))
    for l in range(depth):
        cq, ckv, kr = _latent(h, w_lat, l, g_q[l][None], g_kv[l][None], cs, min(512, s))
        g_mla = _gates(h, w_gmla, l, mw, tm, tn_gate)
        g_mix = _gates(h, w_gmix, l, cw, tm, tn_gate)
        u = _glu(h, w_conv, l, s, tm, tn)
        q = _qup(cq, w_q, l, cs, q_scale, tm, heads_per_tile)
        k, v = _kvup(ckv, w_kv, l, kr, tm, heads_per_tile)
        attn = _attention(q.reshape(b, s, -1), k.reshape(b, s, -1), v.reshape(b, s, -1), g_mla.reshape(b, s, mw),
                          n_heads, tq, tk)
        cc = _conv_module(u, w_slab, bdw_slab, gcn_slab, bcn_slab, g_mix.reshape(b, s, -1), 0, l, ts)
        y = _merge(attn.reshape(t, mw), cc.reshape(t, cw), w_o, w_pw, l, g_mix, cw, cw + d, tm, min(256, d))
        x2 = _outproj(y, w_o2, l, x2, tm, tn)
        if l + 1 < depth:
            h = _rmsnorm(x2, g_pre[l + 1][None], BF16, min(256, t))
    return _rmsnorm(x2, g_final[None], F32, min(256, t)).reshape(b, s, d)


def kernel(x, positions, g_pre, w_in, g_q, w_q_up, g_kv, w_kv_up, w_o_mla, w_dw, b_dw, g_cn, b_cn, w_pw_out, w_out, g_final):
    return _forward(x, positions, g_pre, w_in, g_q, w_q_up, g_kv, w_kv_up, w_o_mla, w_dw, b_dw, g_cn, b_cn,
                    w_pw_out, w_out, g_final, n_heads=N_HEADS)
```

```python
import functools
import math

import jax
import jax.numpy as jnp
from jax import lax
from jax.experimental import pallas as pl
from jax.experimental.pallas import tpu as pltpu

N_HEADS = 32
QK_NOPE = 128
QK_ROPE = 64
V_HEAD = 128
CONV_K = 31
EPS = 1e-6
ROPE_THETA = 10000.0

LANES = 128
HEAD_PAD = QK_NOPE + 2 * QK_ROPE
CONV_HALO = 32
CONV_TAPS_PAD = 32
ATTN_LOOKAHEAD = 2
VMEM_LIMIT_BYTES = 56 * 1024 * 1024

F32 = jnp.float32
BF16 = jnp.bfloat16

assert 2 * QK_ROPE == LANES and QK_NOPE == LANES and V_HEAD == LANES
assert CONV_HALO >= CONV_K - 1


def _params(*sem):
    return pltpu.CompilerParams(dimension_semantics=sem, vmem_limit_bytes=VMEM_LIMIT_BYTES)


def _dot(a, b):
    return jnp.dot(a, b, preferred_element_type=F32)


def _rope_mix(t):
    r = t + pltpu.roll(t, QK_ROPE, 1)
    lane = lax.broadcasted_iota(jnp.int32, r.shape, 1)
    return jnp.where(lane < QK_ROPE, r, 0.0)


def _rmsnorm_body(x_ref, g_ref, o_ref):
    x = x_ref[...]
    ms = jnp.mean(x * x, axis=-1, keepdims=True)
    o_ref[...] = (x * lax.rsqrt(ms + EPS) * g_ref[...]).astype(o_ref.dtype)


def _rmsnorm(x, g_row, out_dtype, tr):
    t, d = x.shape
    return pl.pallas_call(
        _rmsnorm_body,
        grid=(t // tr,),
        in_specs=[pl.BlockSpec((tr, d), lambda i: (i, 0)),
                  pl.BlockSpec((1, d), lambda i: (0, 0))],
        out_specs=pl.BlockSpec((tr, d), lambda i: (i, 0)),
        out_shape=jax.ShapeDtypeStruct((t, d), out_dtype),
        compiler_params=_params("parallel"),
        name="rmsnorm",
    )(x, g_row)


def _rope_table_body(pos_ref, inv_ref, o_ref):
    ang = pos_ref[...].astype(F32) * inv_ref[...]
    lane = lax.broadcasted_iota(jnp.int32, ang.shape, 1)
    c = jnp.cos(ang)
    s = jnp.sin(ang)
    o_ref[...] = jnp.where(lane < QK_ROPE, c, jnp.where(lane < QK_ROPE + QK_ROPE // 2, -s, s))


def _rope_table(pos_col, inv_row, tr):
    t = pos_col.shape[0]
    return pl.pallas_call(
        _rope_table_body,
        grid=(t // tr,),
        in_specs=[pl.BlockSpec((tr, 1), lambda i: (i, 0)),
                  pl.BlockSpec((1, LANES), lambda i: (0, 0))],
        out_specs=pl.BlockSpec((tr, LANES), lambda i: (i, 0)),
        out_shape=jax.ShapeDtypeStruct((t, LANES), F32),
        compiler_params=_params("parallel"),
        name="rope_table",
    )(pos_col, inv_row)


def _latent_body(h_ref, w_ref, gq_ref, gkv_ref, cs_ref, cq_ref, ckv_ref, kr_ref, *, q_lora, kv_lora):
    z = _dot(h_ref[...], w_ref[...])

    def rms(v, g):
        ms = jnp.mean(v * v, axis=-1, keepdims=True)
        return v * lax.rsqrt(ms + EPS) * g

    cq_ref[...] = rms(z[:, :q_lora], gq_ref[...]).astype(BF16)
    ckv_ref[...] = rms(z[:, q_lora:q_lora + kv_lora], gkv_ref[...]).astype(BF16)
    kr_ref[...] = _rope_mix(z[:, q_lora + kv_lora:] * cs_ref[...]).astype(BF16)


def _latent(h, w_lat, layer, gq_row, gkv_row, cs, tm):
    t, d = h.shape
    q_lora, kv_lora = gq_row.shape[1], gkv_row.shape[1]
    nl = w_lat.shape[2]
    return pl.pallas_call(
        functools.partial(_latent_body, q_lora=q_lora, kv_lora=kv_lora),
        grid=(t // tm,),
        in_specs=[pl.BlockSpec((tm, d), lambda i: (i, 0)),
                  pl.BlockSpec((None, d, nl), lambda i: (layer, 0, 0)),
                  pl.BlockSpec((1, q_lora), lambda i: (0, 0)),
                  pl.BlockSpec((1, kv_lora), lambda i: (0, 0)),
                  pl.BlockSpec((tm, LANES), lambda i: (i, 0))],
        out_specs=[pl.BlockSpec((tm, q_lora), lambda i: (i, 0)),
                   pl.BlockSpec((tm, kv_lora), lambda i: (i, 0)),
                   pl.BlockSpec((tm, LANES), lambda i: (i, 0))],
        out_shape=[jax.ShapeDtypeStruct((t, q_lora), BF16),
                   jax.ShapeDtypeStruct((t, kv_lora), BF16),
                   jax.ShapeDtypeStruct((t, LANES), BF16)],
        compiler_params=_params("parallel"),
        name="latent_proj",
    )(h, w_lat, gq_row, gkv_row, cs)


def _gates_body(h_ref, w_ref, o_ref, *, n_silu_tiles):
    z = _dot(h_ref[...], w_ref[...])
    s = jax.nn.sigmoid(z)
    is_silu = pl.program_id(1) < n_silu_tiles
    o_ref[...] = jnp.where(is_silu, z * s, s).astype(BF16)


def _gates(h, w, layer, col0, n, n_silu_cols, tm, tn):
    t, d = h.shape
    j0 = col0 // tn
    return pl.pallas_call(
        functools.partial(_gates_body, n_silu_tiles=n_silu_cols // tn),
        grid=(t // tm, n // tn),
        in_specs=[pl.BlockSpec((tm, d), lambda i, j: (i, 0)),
                  pl.BlockSpec((None, d, tn), lambda i, j: (layer, 0, j0 + j))],
        out_specs=pl.BlockSpec((tm, tn), lambda i, j: (i, j)),
        out_shape=jax.ShapeDtypeStruct((t, n), BF16),
        compiler_params=_params("parallel", "arbitrary"),
        name="gate_proj",
    )(h, w)


def _glu_body(h_ref, wv_ref, wg_ref, o_ref):
    h = h_ref[...]
    u = _dot(h, wv_ref[...]) * jax.nn.sigmoid(_dot(h, wg_ref[...]))
    for c in range(o_ref.shape[0]):
        o_ref[c] = u[:, c * LANES:(c + 1) * LANES]


def _glu(h, w, layer, col0, cw, seq, tm, tn):
    t, d = h.shape
    nsb = seq // tm
    nj = cw // tn
    j0 = col0 // tn
    return pl.pallas_call(
        _glu_body,
        grid=(t // tm, nj),
        in_specs=[pl.BlockSpec((tm, d), lambda i, j: (i, 0)),
                  pl.BlockSpec((None, d, tn), lambda i, j: (layer, 0, j0 + j)),
                  pl.BlockSpec((None, d, tn), lambda i, j: (layer, 0, j0 + nj + j))],
        out_specs=pl.BlockSpec((None, tn // LANES, tm, LANES), lambda i, j: (i // nsb, j, i % nsb, 0)),
        out_shape=jax.ShapeDtypeStruct((t // seq, cw // LANES, seq, LANES), F32),
        compiler_params=_params("parallel", "arbitrary"),
        name="glu_proj",
    )(h, w, w)


def _qup_body(a_ref, w_ref, cs_ref, o_ref, *, heads, scale):
    y = _dot(a_ref[...], w_ref[...])
    cs = cs_ref[...]
    for i in range(heads):
        c0 = i * HEAD_PAD
        o_ref[:, c0:c0 + QK_NOPE] = (y[:, c0:c0 + QK_NOPE] * scale).astype(BF16)
        o_ref[:, c0 + QK_NOPE:c0 + HEAD_PAD] = (_rope_mix(y[:, c0 + QK_NOPE:c0 + HEAD_PAD] * cs) * scale).astype(BF16)


def _qup(cq, w_q, layer, cs, scale, tm, heads):
    t, ql = cq.shape
    n = w_q.shape[2]
    tn = heads * HEAD_PAD
    return pl.pallas_call(
        functools.partial(_qup_body, heads=heads, scale=scale),
        grid=(t // tm, n // tn),
        in_specs=[pl.BlockSpec((tm, ql), lambda i, j: (i, 0)),
                  pl.BlockSpec((None, ql, tn), lambda i, j: (layer, 0, j)),
                  pl.BlockSpec((tm, LANES), lambda i, j: (i, 0))],
        out_specs=pl.BlockSpec((tm, tn), lambda i, j: (i, j)),
        out_shape=jax.ShapeDtypeStruct((t, n), BF16),
        compiler_params=_params("parallel", "arbitrary"),
        name="q_up",
    )(cq, w_q, cs)


def _kvup_body(a_ref, w_ref, kr_ref, k_ref, v_ref, *, heads):
    y = _dot(a_ref[...], w_ref[...])
    kr = kr_ref[...]
    for i in range(heads):
        c0 = i * HEAD_PAD
        k_ref[:, c0:c0 + QK_NOPE] = y[:, c0:c0 + QK_NOPE].astype(BF16)
        k_ref[:, c0 + QK_NOPE:c0 + HEAD_PAD] = kr
        v_ref[:, i * V_HEAD:(i + 1) * V_HEAD] = y[:, c0 + QK_NOPE:c0 + QK_NOPE + V_HEAD].astype(BF16)


def _kvup(ckv, w_kv, layer, kr, tm, heads):
    t, kvl = ckv.shape
    n_heads = w_kv.shape[2] // (QK_NOPE + V_HEAD)
    tn = heads * (QK_NOPE + V_HEAD)
    return pl.pallas_call(
        functools.partial(_kvup_body, heads=heads),
        grid=(t // tm, n_heads // heads),
        in_specs=[pl.BlockSpec((tm, kvl), lambda i, j: (i, 0)),
                  pl.BlockSpec((None, kvl, tn), lambda i, j: (layer, 0, j)),
                  pl.BlockSpec((tm, LANES), lambda i, j: (i, 0))],
        out_specs=[pl.BlockSpec((tm, heads * HEAD_PAD), lambda i, j: (i, j)),
                   pl.BlockSpec((tm, heads * V_HEAD), lambda i, j: (i, j))],
        out_shape=[jax.ShapeDtypeStruct((t, n_heads * HEAD_PAD), BF16),
                   jax.ShapeDtypeStruct((t, n_heads * V_HEAD), BF16)],
        compiler_params=_params("parallel", "arbitrary"),
        name="kv_up",
    )(ckv, w_kv, kr)


def _attn_body(q_ref, k_ref, v_ref, g_ref, o_ref, vt_ref, qt_ref, *, tq, tk):
    qi = pl.program_id(2)
    n_kv = v_ref.shape[0] // tk
    cpq = tq // tk

    @pl.when(qi == 0)
    def _():
        for c in range(n_kv):
            vt_ref[c] = v_ref[c * tk:(c + 1) * tk, :].T

    qt_ref[...] = q_ref[...].T

    def scores(c, q0):
        return _dot(k_ref[c * tk:(c + 1) * tk, :], qt_ref[:, q0:])

    def update(c, st, state, q0, masked):
        m, l, acc = (x[:, q0:] for x in state)
        if masked:
            key = lax.broadcasted_iota(jnp.int32, st.shape, 0)
            qry = lax.broadcasted_iota(jnp.int32, st.shape, 1)
            st = jnp.where(key <= qry, st, -1e30)
        m_new = jnp.maximum(m, jnp.max(st, axis=0, keepdims=True))
        alpha = jnp.exp2(m - m_new)
        pt = jnp.exp2(st - m_new)
        l = alpha * l + jnp.sum(pt, axis=0, keepdims=True)
        acc = alpha * acc + _dot(vt_ref[c], pt.astype(BF16))
        if q0 == 0:
            return m_new, l, acc
        return tuple(jnp.concatenate([old[:, :q0], new], axis=1) for old, new in zip(state, (m_new, l, acc)))

    def q_tile_program(t):
        n_chunks = (t + 1) * cpq
        q0_of = lambda c: max(c - t * cpq, 0) * tk
        sts = {c: scores(c, q0_of(c)) for c in range(min(ATTN_LOOKAHEAD, n_chunks))}
        state = (jnp.full((1, tq), -1e30, F32), jnp.zeros((1, tq), F32), jnp.zeros((V_HEAD, tq), F32))
        for c in range(n_chunks):
            state = update(c, sts.pop(c), state, q0_of(c), c >= t * cpq)
            if c + ATTN_LOOKAHEAD < n_chunks:
                sts[c + ATTN_LOOKAHEAD] = scores(c + ATTN_LOOKAHEAD, q0_of(c + ATTN_LOOKAHEAD))
        _, l, acc = state
        out = (acc * (1.0 / l)).T
        o_ref[...] = (out * g_ref[...].astype(F32)).astype(BF16)

    for t in range(n_kv // cpq):
        pl.when(qi == t)(functools.partial(q_tile_program, t))


def _attention(q, k, v, gates, n_heads, tq, tk):
    b, s, _ = q.shape
    return pl.pallas_call(
        functools.partial(_attn_body, tq=tq, tk=tk),
        grid=(b, n_heads, s // tq),
        scratch_shapes=[pltpu.VMEM((s // tk, V_HEAD, tk), BF16),
                        pltpu.VMEM((HEAD_PAD, tq), BF16)],
        in_specs=[pl.BlockSpec((None, tq, HEAD_PAD), lambda bi, h, i: (bi, i, h)),
                  pl.BlockSpec((None, s, HEAD_PAD), lambda bi, h, i: (bi, 0, h)),
                  pl.BlockSpec((None, s, V_HEAD), lambda bi, h, i: (bi, 0, h)),
                  pl.BlockSpec((None, tq, V_HEAD), lambda bi, h, i: (bi, i, h))],
        out_specs=pl.BlockSpec((None, tq, V_HEAD), lambda bi, h, i: (bi, i, h)),
        out_shape=jax.ShapeDtypeStruct((b, s, n_heads * V_HEAD), BF16),
        compiler_params=_params("parallel", "parallel", "arbitrary"),
        name="mla_attention",
    )(q, k, v, gates)


def _conv_body(u_ref, halo_ref, w_ref, bdw_ref, gcn_ref, bcn_ref, gate_ref, o_ref, win_ref, y_ref,
               *, ts, rc, rf):
    nc = u_ref.shape[0]
    channels = nc * LANES
    i = pl.program_id(1)

    @pl.when(i == 0)
    def _():
        win_ref[:, 0:CONV_HALO, :] = jnp.zeros((nc, CONV_HALO, LANES), F32)

    @pl.when(i > 0)
    def _():
        win_ref[:, 0:CONV_HALO, :] = halo_ref[...]

    win_ref[:, CONV_HALO:CONV_HALO + ts, :] = u_ref[...]
    base = CONV_HALO - (CONV_K - 1)

    def conv_chunk(c, _):
        for r0 in range(0, ts, rc):
            acc = jnp.broadcast_to(bdw_ref[c], (rc, LANES))
            for k in range(CONV_K):
                acc = acc + win_ref[c, pl.ds(base + r0 + k, rc), :] * w_ref[c, k:k + 1, :]
            y_ref[c, pl.ds(r0, rc), :] = acc
        return 0

    lax.fori_loop(0, nc, conv_chunk, 0)

    inv_c = 1.0 / channels
    for r0 in range(0, ts, rf):
        tot = y_ref[0, r0:r0 + rf, :]
        for c in range(1, nc):
            tot = tot + y_ref[c, r0:r0 + rf, :]
        mu = jnp.broadcast_to(jnp.sum(tot, axis=-1, keepdims=True) * inv_c, (rf, LANES))
        sq = None
        for c in range(nc):
            d = y_ref[c, r0:r0 + rf, :] - mu
            sq = d * d if sq is None else sq + d * d
        rstd = jnp.broadcast_to(lax.rsqrt(jnp.sum(sq, axis=-1, keepdims=True) * inv_c + EPS), (rf, LANES))
        for c in range(nc):
            z = (y_ref[c, r0:r0 + rf, :] - mu) * rstd * gcn_ref[c] + bcn_ref[c]
            z = z * jax.nn.sigmoid(z)
            g = gate_ref[r0:r0 + rf, c * LANES:(c + 1) * LANES].astype(F32)
            o_ref[r0:r0 + rf, c * LANES:(c + 1) * LANES] = (z * g).astype(BF16)


def _conv_module(u, w_slab, bdw_slab, gcn_slab, bcn_slab, gates, gate_col0, layer, ts):
    b, nc, s, _ = u.shape
    c = nc * LANES
    hb = ts // CONV_HALO
    gblk = gate_col0 // c
    return pl.pallas_call(
        functools.partial(_conv_body, ts=ts, rc=min(64, ts), rf=min(64, ts)),
        grid=(b, s // ts),
        in_specs=[pl.BlockSpec((None, nc, ts, LANES), lambda bi, i: (bi, 0, i, 0)),
                  pl.BlockSpec((None, nc, CONV_HALO, LANES),
                               lambda bi, i: (bi, 0, jnp.maximum(i * hb - 1, 0), 0)),
                  pl.BlockSpec((None, nc, CONV_TAPS_PAD, LANES), lambda bi, i: (layer, 0, 0, 0)),
                  pl.BlockSpec((None, nc, 1, LANES), lambda bi, i: (layer, 0, 0, 0)),
                  pl.BlockSpec((None, nc, 1, LANES), lambda bi, i: (layer, 0, 0, 0)),
                  pl.BlockSpec((None, nc, 1, LANES), lambda bi, i: (layer, 0, 0, 0)),
                  pl.BlockSpec((None, ts, c), lambda bi, i: (bi, i, gblk))],
        out_specs=pl.BlockSpec((None, ts, c), lambda bi, i: (bi, i, 0)),
        out_shape=jax.ShapeDtypeStruct((b, s, c), BF16),
        scratch_shapes=[pltpu.VMEM((nc, CONV_HALO + ts, LANES), F32),
                        pltpu.VMEM((nc, ts, LANES), F32)],
        compiler_params=_params("parallel", "arbitrary"),
        name="conv_module",
    )(u, u, w_slab, bdw_slab, gcn_slab, bcn_slab, gates)


def _merge_body(a_ref, c_ref, wo_ref, wpw_ref, sa_ref, sc_ref, o_ref):
    ya = _dot(a_ref[...], wo_ref[...])
    yc = _dot(c_ref[...], wpw_ref[...])
    o_ref[...] = (sa_ref[...].astype(F32) * ya + sc_ref[...].astype(F32) * yc).astype(BF16)


def _merge(a, cc, w_o, w_pw, layer, gates, col_a, col_c, tm, tn):
    t, mw = a.shape
    cw = cc.shape[1]
    d = w_o.shape[2]
    ja, jc = col_a // tn, col_c // tn
    return pl.pallas_call(
        _merge_body,
        grid=(t // tm, d // tn),
        in_specs=[pl.BlockSpec((tm, mw), lambda i, j: (i, 0)),
                  pl.BlockSpec((tm, cw), lambda i, j: (i, 0)),
                  pl.BlockSpec((None, mw, tn), lambda i, j: (layer, 0, j)),
                  pl.BlockSpec((None, cw, tn), lambda i, j: (layer, 0, j)),
                  pl.BlockSpec((tm, tn), lambda i, j: (i, ja + j)),
                  pl.BlockSpec((tm, tn), lambda i, j: (i, jc + j))],
        out_specs=pl.BlockSpec((tm, tn), lambda i, j: (i, j)),
        out_shape=jax.ShapeDtypeStruct((t, d), BF16),
        compiler_params=_params("parallel", "arbitrary"),
        name="merge_proj",
    )(a, cc, w_o, w_pw, gates, gates)


def _outproj_body(y_ref, w_ref, x_ref, o_ref):
    o_ref[...] = x_ref[...] + _dot(y_ref[...], w_ref[...])


def _outproj(y, w_out, layer, x, tm, tn):
    t, d = y.shape
    n = w_out.shape[2]
    return pl.pallas_call(
        _outproj_body,
        grid=(t // tm, n // tn),
        in_specs=[pl.BlockSpec((tm, d), lambda i, j: (i, 0)),
                  pl.BlockSpec((None, d, tn), lambda i, j: (layer, 0, j)),
                  pl.BlockSpec((tm, tn), lambda i, j: (i, j))],
        out_specs=pl.BlockSpec((tm, tn), lambda i, j: (i, j)),
        out_shape=jax.ShapeDtypeStruct((t, n), F32),
        compiler_params=_params("parallel", "arbitrary"),
        name="out_proj",
    )(y, w_out, x)


def _repack_body(w_ref, o_ref, *, col0):
    o_ref[...] = w_ref[:, col0:].astype(BF16)


def _repack_tail(w, col0, tr):
    depth, d, n = w.shape
    return pl.pallas_call(
        functools.partial(_repack_body, col0=col0),
        grid=(depth, d // tr),
        in_specs=[pl.BlockSpec((None, tr, n), lambda l, i: (l, i, 0))],
        out_specs=pl.BlockSpec((None, tr, n - col0), lambda l, i: (l, i, 0)),
        out_shape=jax.ShapeDtypeStruct((depth, d, n - col0), BF16),
        compiler_params=_params("parallel", "parallel"),
        name="w_in_repack",
    )(w)


def _slab(p):
    l, r, c = p.shape
    return p.reshape(l, r, c // LANES, LANES).transpose(0, 2, 1, 3)


def _forward(x, positions, g_pre, w_in, g_q, w_q_up, g_kv, w_kv_up, w_o_mla, w_dw, b_dw, g_cn, b_cn,
             w_pw_out, w_out, g_final, *, n_heads):
    b, s, d = x.shape
    depth = g_pre.shape[0]
    t = b * s
    q_lora, kv_lora = g_q.shape[1], g_kv.shape[1]
    mw = n_heads * V_HEAD
    cw = w_dw.shape[2]
    off_kr = q_lora + kv_lora
    off_gmla = off_kr + QK_ROPE
    off_conv = off_gmla + mw
    off_gconv = off_conv + 2 * cw
    assert w_in.shape[2] == off_gconv + cw + 2 * d and cw == d

    tm = min(1024, s)
    tn = min(512, d)
    tn_gate = min(1024, d)
    tq = min(1024, s)
    tk = min(256, s)
    ts = min(256, s)
    heads_per_tile = min(8, n_heads)

    swap = jnp.concatenate([jnp.arange(QK_ROPE // 2, QK_ROPE), jnp.arange(0, QK_ROPE // 2)])
    w_kr = w_in[:, :, off_kr:off_gmla]
    w_lat = jnp.concatenate([w_in[:, :, :off_gmla], w_kr[:, :, swap]], axis=2).astype(BF16)
    w_tail = _repack_tail(w_in, off_gmla, min(64, d))
    wq = w_q_up.reshape(depth, q_lora, n_heads, QK_NOPE + QK_ROPE)
    w_q = jnp.concatenate([wq, wq[..., QK_NOPE:][..., swap]], axis=3).reshape(depth, q_lora, n_heads * HEAD_PAD)
    w_q = w_q.astype(BF16)
    w_kv = w_kv_up.astype(BF16)
    w_o = w_o_mla.astype(BF16)
    w_pw = w_pw_out.astype(BF16)
    w_o2 = w_out.astype(BF16)
    w_slab = _slab(jnp.pad(w_dw, ((0, 0), (0, CONV_TAPS_PAD - CONV_K), (0, 0))))
    bdw_slab, gcn_slab, bcn_slab = _slab(b_dw[:, None]), _slab(g_cn[:, None]), _slab(b_cn[:, None])

    inv_freq = 1.0 / (ROPE_THETA ** (jnp.arange(0, QK_ROPE, 2, dtype=F32) / QK_ROPE))
    cs = _rope_table(positions.reshape(t, 1), jnp.tile(inv_freq, 4)[None], min(1024, t))

    q_scale = (1.0 / math.sqrt(QK_NOPE + QK_ROPE)) * math.log2(math.e)
    x2 = x.reshape(t, d)
    h = _rmsnorm(x2, g_pre[0][None], BF16, min(256, t))
    for l in range(depth):
        cq, ckv, kr = _latent(h, w_lat, l, g_q[l][None], g_kv[l][None], cs, min(512, s))
        g_mla = _gates(h, w_tail, l, 0, mw, mw, tm, tn_gate)
        g_mix = _gates(h, w_tail, l, mw + 2 * cw, cw + 2 * d, cw, tm, tn_gate)
        u = _glu(h, w_tail, l, mw, cw, s, tm, tn)
        q = _qup(cq, w_q, l, cs, q_scale, tm, heads_per_tile)
        k, v = _kvup(ckv, w_kv, l, kr, tm, heads_per_tile)
        attn = _attention(q.reshape(b, s, -1), k.reshape(b, s, -1), v.reshape(b, s, -1), g_mla.reshape(b, s, mw),
                          n_heads, tq, tk)
        cc = _conv_module(u, w_slab, bdw_slab, gcn_slab, bcn_slab, g_mix.reshape(b, s, -1), 0, l, ts)
        y = _merge(attn.reshape(t, mw), cc.reshape(t, cw), w_o, w_pw, l, g_mix, cw, cw + d, tm, min(256, d))
        x2 = _outproj(y, w_o2, l, x2, tm, tn)
        if l + 1 < depth:
            h = _rmsnorm(x2, g_pre[l + 1][None], BF16, min(256, t))
    return _rmsnorm(x2, g_final[None], F32, min(256, t)).reshape(b, s, d)


def kernel(x, positions, g_pre, w_in, g_q, w_q_up, g_kv, w_kv_up, w_o_mla, w_dw, b_dw, g_cn, b_cn, w_pw_out, w_out, g_final):
    return _forward(x, positions, g_pre, w_in, g_q, w_q_up, g_kv, w_kv_up, w_o_mla, w_dw, b_dw, g_cn, b_cn,
                    w_pw_out, w_out, g_final, n_heads=N_HEADS)
```

```python
import functools
import math

import jax
import jax.numpy as jnp
from jax import lax
from jax.experimental import pallas as pl
from jax.experimental.pallas import tpu as pltpu

N_HEADS = 32
QK_NOPE = 128
QK_ROPE = 64
V_HEAD = 128
CONV_K = 31
EPS = 1e-6
ROPE_THETA = 10000.0

LANES = 128
HEAD_PAD = QK_NOPE + 2 * QK_ROPE
CONV_HALO = 32
CONV_TAPS_PAD = 32
ATTN_LOOKAHEAD = 2
VMEM_LIMIT_BYTES = 56 * 1024 * 1024

F32 = jnp.float32
BF16 = jnp.bfloat16

assert 2 * QK_ROPE == LANES and QK_NOPE == LANES and V_HEAD == LANES
assert CONV_HALO >= CONV_K - 1


def _params(*sem):
    return pltpu.CompilerParams(dimension_semantics=sem, vmem_limit_bytes=VMEM_LIMIT_BYTES)


def _dot(a, b):
    return jnp.dot(a, b, preferred_element_type=F32)


def _rope_mix(t):
    r = t + pltpu.roll(t, QK_ROPE, 1)
    lane = lax.broadcasted_iota(jnp.int32, r.shape, 1)
    return jnp.where(lane < QK_ROPE, r, 0.0)


def _rmsnorm_body(x_ref, g_ref, o_ref):
    x = x_ref[...]
    ms = jnp.mean(x * x, axis=-1, keepdims=True)
    o_ref[...] = (x * lax.rsqrt(ms + EPS) * g_ref[...]).astype(o_ref.dtype)


def _rmsnorm(x, g_row, out_dtype, tr):
    t, d = x.shape
    return pl.pallas_call(
        _rmsnorm_body,
        grid=(t // tr,),
        in_specs=[pl.BlockSpec((tr, d), lambda i: (i, 0)),
                  pl.BlockSpec((1, d), lambda i: (0, 0))],
        out_specs=pl.BlockSpec((tr, d), lambda i: (i, 0)),
        out_shape=jax.ShapeDtypeStruct((t, d), out_dtype),
        compiler_params=_params("parallel"),
        name="rmsnorm",
    )(x, g_row)


def _rope_table_body(pos_ref, inv_ref, o_ref):
    ang = pos_ref[...].astype(F32) * inv_ref[...]
    lane = lax.broadcasted_iota(jnp.int32, ang.shape, 1)
    c = jnp.cos(ang)
    s = jnp.sin(ang)
    o_ref[...] = jnp.where(lane < QK_ROPE, c, jnp.where(lane < QK_ROPE + QK_ROPE // 2, -s, s))


def _rope_table(pos_col, inv_row, tr):
    t = pos_col.shape[0]
    return pl.pallas_call(
        _rope_table_body,
        grid=(t // tr,),
        in_specs=[pl.BlockSpec((tr, 1), lambda i: (i, 0)),
                  pl.BlockSpec((1, LANES), lambda i: (0, 0))],
        out_specs=pl.BlockSpec((tr, LANES), lambda i: (i, 0)),
        out_shape=jax.ShapeDtypeStruct((t, LANES), F32),
        compiler_params=_params("parallel"),
        name="rope_table",
    )(pos_col, inv_row)


def _latent_body(h_ref, w_ref, gq_ref, gkv_ref, cs_ref, cq_ref, ckv_ref, kr_ref, *, q_lora, kv_lora):
    z = _dot(h_ref[...], w_ref[...])

    def rms(v, g):
        ms = jnp.mean(v * v, axis=-1, keepdims=True)
        return v * lax.rsqrt(ms + EPS) * g

    cq_ref[...] = rms(z[:, :q_lora], gq_ref[...]).astype(BF16)
    ckv_ref[...] = rms(z[:, q_lora:q_lora + kv_lora], gkv_ref[...]).astype(BF16)
    kr_ref[...] = _rope_mix(z[:, q_lora + kv_lora:] * cs_ref[...]).astype(BF16)


def _latent(h, w_lat, layer, gq_row, gkv_row, cs, tm):
    t, d = h.shape
    q_lora, kv_lora = gq_row.shape[1], gkv_row.shape[1]
    nl = w_lat.shape[2]
    return pl.pallas_call(
        functools.partial(_latent_body, q_lora=q_lora, kv_lora=kv_lora),
        grid=(t // tm,),
        in_specs=[pl.BlockSpec((tm, d), lambda i: (i, 0)),
                  pl.BlockSpec((None, d, nl), lambda i: (layer, 0, 0)),
                  pl.BlockSpec((1, q_lora), lambda i: (0, 0)),
                  pl.BlockSpec((1, kv_lora), lambda i: (0, 0)),
                  pl.BlockSpec((tm, LANES), lambda i: (i, 0))],
        out_specs=[pl.BlockSpec((tm, q_lora), lambda i: (i, 0)),
                   pl.BlockSpec((tm, kv_lora), lambda i: (i, 0)),
                   pl.BlockSpec((tm, LANES), lambda i: (i, 0))],
        out_shape=[jax.ShapeDtypeStruct((t, q_lora), BF16),
                   jax.ShapeDtypeStruct((t, kv_lora), BF16),
                   jax.ShapeDtypeStruct((t, LANES), BF16)],
        compiler_params=_params("parallel"),
        name="latent_proj",
    )(h, w_lat, gq_row, gkv_row, cs)


def _gates_body(h_ref, w_ref, o_ref, *, n_silu_tiles):
    z = _dot(h_ref[...], w_ref[...])
    s = jax.nn.sigmoid(z)
    is_silu = pl.program_id(1) < n_silu_tiles
    o_ref[...] = jnp.where(is_silu, z * s, s).astype(BF16)


def _gates(h, w, layer, col0, n, n_silu_cols, tm, tn):
    t, d = h.shape
    j0 = col0 // tn
    return pl.pallas_call(
        functools.partial(_gates_body, n_silu_tiles=n_silu_cols // tn),
        grid=(t // tm, n // tn),
        in_specs=[pl.BlockSpec((tm, d), lambda i, j: (i, 0)),
                  pl.BlockSpec((None, d, tn), lambda i, j: (layer, 0, j0 + j))],
        out_specs=pl.BlockSpec((tm, tn), lambda i, j: (i, j)),
        out_shape=jax.ShapeDtypeStruct((t, n), BF16),
        compiler_params=_params("parallel", "arbitrary"),
        name="gate_proj",
    )(h, w)


def _glu_body(h_ref, wv_ref, wg_ref, o_ref):
    h = h_ref[...]
    u = _dot(h, wv_ref[...]) * jax.nn.sigmoid(_dot(h, wg_ref[...]))
    for c in range(o_ref.shape[0]):
        o_ref[c] = u[:, c * LANES:(c + 1) * LANES]


def _glu(h, w, layer, col0, cw, seq, tm, tn):
    t, d = h.shape
    nsb = seq // tm
    nj = cw // tn
    j0 = col0 // tn
    return pl.pallas_call(
        _glu_body,
        grid=(t // tm, nj),
        in_specs=[pl.BlockSpec((tm, d), lambda i, j: (i, 0)),
                  pl.BlockSpec((None, d, tn), lambda i, j: (layer, 0, j0 + j)),
                  pl.BlockSpec((None, d, tn), lambda i, j: (layer, 0, j0 + nj + j))],
        out_specs=pl.BlockSpec((None, tn // LANES, tm, LANES), lambda i, j: (i // nsb, j, i % nsb, 0)),
        out_shape=jax.ShapeDtypeStruct((t // seq, cw // LANES, seq, LANES), F32),
        compiler_params=_params("parallel", "arbitrary"),
        name="glu_proj",
    )(h, w, w)


def _qup_body(a_ref, w_ref, cs_ref, o_ref, *, heads, scale):
    y = _dot(a_ref[...], w_ref[...])
    cs = cs_ref[...]
    for i in range(heads):
        c0 = i * HEAD_PAD
        o_ref[:, c0:c0 + QK_NOPE] = (y[:, c0:c0 + QK_NOPE] * scale).astype(BF16)
        o_ref[:, c0 + QK_NOPE:c0 + HEAD_PAD] = (_rope_mix(y[:, c0 + QK_NOPE:c0 + HEAD_PAD] * cs) * scale).astype(BF16)


def _qup(cq, w_q, layer, cs, scale, tm, heads):
    t, ql = cq.shape
    n = w_q.shape[2]
    tn = heads * HEAD_PAD
    return pl.pallas_call(
        functools.partial(_qup_body, heads=heads, scale=scale),
        grid=(t // tm, n // tn),
        in_specs=[pl.BlockSpec((tm, ql), lambda i, j: (i, 0)),
                  pl.BlockSpec((None, ql, tn), lambda i, j: (layer, 0, j)),
                  pl.BlockSpec((tm, LANES), lambda i, j: (i, 0))],
        out_specs=pl.BlockSpec((tm, tn), lambda i, j: (i, j)),
        out_shape=jax.ShapeDtypeStruct((t, n), BF16),
        compiler_params=_params("parallel", "arbitrary"),
        name="q_up",
    )(cq, w_q, cs)


def _kvup_body(a_ref, w_ref, kr_ref, k_ref, v_ref, *, heads):
    y = _dot(a_ref[...], w_ref[...])
    kr = kr_ref[...]
    for i in range(heads):
        c0 = i * HEAD_PAD
        k_ref[:, c0:c0 + QK_NOPE] = y[:, c0:c0 + QK_NOPE].astype(BF16)
        k_ref[:, c0 + QK_NOPE:c0 + HEAD_PAD] = kr
        v_ref[:, i * V_HEAD:(i + 1) * V_HEAD] = y[:, c0 + QK_NOPE:c0 + QK_NOPE + V_HEAD].astype(BF16)


def _kvup(ckv, w_kv, layer, kr, tm, heads):
    t, kvl = ckv.shape
    n_heads = w_kv.shape[2] // (QK_NOPE + V_HEAD)
    tn = heads * (QK_NOPE + V_HEAD)
    return pl.pallas_call(
        functools.partial(_kvup_body, heads=heads),
        grid=(t // tm, n_heads // heads),
        in_specs=[pl.BlockSpec((tm, kvl), lambda i, j: (i, 0)),
                  pl.BlockSpec((None, kvl, tn), lambda i, j: (layer, 0, j)),
                  pl.BlockSpec((tm, LANES), lambda i, j: (i, 0))],
        out_specs=[pl.BlockSpec((tm, heads * HEAD_PAD), lambda i, j: (i, j)),
                   pl.BlockSpec((tm, heads * V_HEAD), lambda i, j: (i, j))],
        out_shape=[jax.ShapeDtypeStruct((t, n_heads * HEAD_PAD), BF16),
                   jax.ShapeDtypeStruct((t, n_heads * V_HEAD), BF16)],
        compiler_params=_params("parallel", "arbitrary"),
        name="kv_up",
    )(ckv, w_kv, kr)


def _attn_body(q_ref, k_ref, v_ref, g_ref, o_ref, vt_ref, qt_ref, *, tq, tk):
    qi = pl.program_id(2)
    n_kv = v_ref.shape[0] // tk
    cpq = tq // tk

    @pl.when(qi == 0)
    def _():
        for c in range(n_kv):
            vt_ref[c] = v_ref[c * tk:(c + 1) * tk, :].T

    qt_ref[...] = q_ref[...].T

    def scores(c, q0):
        return _dot(k_ref[c * tk:(c + 1) * tk, :], qt_ref[:, q0:])

    def update(c, st, state, q0, masked):
        m, l, acc = (x[:, q0:] for x in state)
        if masked:
            key = lax.broadcasted_iota(jnp.int32, st.shape, 0)
            qry = lax.broadcasted_iota(jnp.int32, st.shape, 1)
            st = jnp.where(key <= qry, st, -1e30)
        m_new = jnp.maximum(m, jnp.max(st, axis=0, keepdims=True))
        alpha = jnp.exp2(m - m_new)
        pt = jnp.exp2(st - m_new)
        l = alpha * l + jnp.sum(pt, axis=0, keepdims=True)
        acc = alpha * acc + _dot(vt_ref[c], pt.astype(BF16))
        if q0 == 0:
            return m_new, l, acc
        return tuple(jnp.concatenate([old[:, :q0], new], axis=1) for old, new in zip(state, (m_new, l, acc)))

    def q_tile_program(t):
        n_chunks = (t + 1) * cpq
        q0_of = lambda c: max(c - t * cpq, 0) * tk
        sts = {c: scores(c, q0_of(c)) for c in range(min(ATTN_LOOKAHEAD, n_chunks))}
        state = (jnp.full((1, tq), -1e30, F32), jnp.zeros((1, tq), F32), jnp.zeros((V_HEAD, tq), F32))
        for c in range(n_chunks):
            state = update(c, sts.pop(c), state, q0_of(c), c >= t * cpq)
            if c + ATTN_LOOKAHEAD < n_chunks:
                sts[c + ATTN_LOOKAHEAD] = scores(c + ATTN_LOOKAHEAD, q0_of(c + ATTN_LOOKAHEAD))
        _, l, acc = state
        out = (acc * (1.0 / l)).T
        o_ref[...] = (out * g_ref[...].astype(F32)).astype(BF16)

    for t in range(n_kv // cpq):
        pl.when(qi == t)(functools.partial(q_tile_program, t))


def _attention(q, k, v, gates, n_heads, tq, tk):
    b, s, _ = q.shape
    return pl.pallas_call(
        functools.partial(_attn_body, tq=tq, tk=tk),
        grid=(b, n_heads, s // tq),
        scratch_shapes=[pltpu.VMEM((s // tk, V_HEAD, tk), BF16),
                        pltpu.VMEM((HEAD_PAD, tq), BF16)],
        in_specs=[pl.BlockSpec((None, tq, HEAD_PAD), lambda bi, h, i: (bi, i, h)),
                  pl.BlockSpec((None, s, HEAD_PAD), lambda bi, h, i: (bi, 0, h)),
                  pl.BlockSpec((None, s, V_HEAD), lambda bi, h, i: (bi, 0, h)),
                  pl.BlockSpec((None, tq, V_HEAD), lambda bi, h, i: (bi, i, h))],
        out_specs=pl.BlockSpec((None, tq, V_HEAD), lambda bi, h, i: (bi, i, h)),
        out_shape=jax.ShapeDtypeStruct((b, s, n_heads * V_HEAD), BF16),
        compiler_params=_params("parallel", "parallel", "arbitrary"),
        name="mla_attention",
    )(q, k, v, gates)


def _conv_body(u_ref, halo_ref, w_ref, bdw_ref, gcn_ref, bcn_ref, gate_ref, o_ref, win_ref, y_ref,
               *, ts, rc, rf):
    nc = u_ref.shape[0]
    channels = nc * LANES
    i = pl.program_id(1)

    @pl.when(i == 0)
    def _():
        win_ref[:, 0:CONV_HALO, :] = jnp.zeros((nc, CONV_HALO, LANES), F32)

    @pl.when(i > 0)
    def _():
        win_ref[:, 0:CONV_HALO, :] = halo_ref[...]

    win_ref[:, CONV_HALO:CONV_HALO + ts, :] = u_ref[...]
    base = CONV_HALO - (CONV_K - 1)

    def conv_chunk(c, _):
        for r0 in range(0, ts, rc):
            acc = jnp.broadcast_to(bdw_ref[c], (rc, LANES))
            for k in range(CONV_K):
                acc = acc + win_ref[c, pl.ds(base + r0 + k, rc), :] * w_ref[c, k:k + 1, :]
            y_ref[c, pl.ds(r0, rc), :] = acc
        return 0

    lax.fori_loop(0, nc, conv_chunk, 0)

    inv_c = 1.0 / channels
    for r0 in range(0, ts, rf):
        tot = y_ref[0, r0:r0 + rf, :]
        for c in range(1, nc):
            tot = tot + y_ref[c, r0:r0 + rf, :]
        mu = jnp.broadcast_to(jnp.sum(tot, axis=-1, keepdims=True) * inv_c, (rf, LANES))
        sq = None
        for c in range(nc):
            d = y_ref[c, r0:r0 + rf, :] - mu
            sq = d * d if sq is None else sq + d * d
        rstd = jnp.broadcast_to(lax.rsqrt(jnp.sum(sq, axis=-1, keepdims=True) * inv_c + EPS), (rf, LANES))
        for c in range(nc):
            z = (y_ref[c, r0:r0 + rf, :] - mu) * rstd * gcn_ref[c] + bcn_ref[c]
            z = z * jax.nn.sigmoid(z)
            g = gate_ref[r0:r0 + rf, c * LANES:(c + 1) * LANES].astype(F32)
            o_ref[r0:r0 + rf, c * LANES:(c + 1) * LANES] = (z * g).astype(BF16)


def _conv_module(u, w_slab, bdw_slab, gcn_slab, bcn_slab, gates, gate_col0, layer, ts):
    b, nc, s, _ = u.shape
    c = nc * LANES
    hb = ts // CONV_HALO
    gblk = gate_col0 // c
    return pl.pallas_call(
        functools.partial(_conv_body, ts=ts, rc=min(64, ts), rf=min(64, ts)),
        grid=(b, s // ts),
        in_specs=[pl.BlockSpec((None, nc, ts, LANES), lambda bi, i: (bi, 0, i, 0)),
                  pl.BlockSpec((None, nc, CONV_HALO, LANES),
                               lambda bi, i: (bi, 0, jnp.maximum(i * hb - 1, 0), 0)),
                  pl.BlockSpec((None, nc, CONV_TAPS_PAD, LANES), lambda bi, i: (layer, 0, 0, 0)),
                  pl.BlockSpec((None, nc, 1, LANES), lambda bi, i: (layer, 0, 0, 0)),
                  pl.BlockSpec((None, nc, 1, LANES), lambda bi, i: (layer, 0, 0, 0)),
                  pl.BlockSpec((None, nc, 1, LANES), lambda bi, i: (layer, 0, 0, 0)),
                  pl.BlockSpec((None, ts, c), lambda bi, i: (bi, i, gblk))],
        out_specs=pl.BlockSpec((None, ts, c), lambda bi, i: (bi, i, 0)),
        out_shape=jax.ShapeDtypeStruct((b, s, c), BF16),
        scratch_shapes=[pltpu.VMEM((nc, CONV_HALO + ts, LANES), F32),
                        pltpu.VMEM((nc, ts, LANES), F32)],
        compiler_params=_params("parallel", "arbitrary"),
        name="conv_module",
    )(u, u, w_slab, bdw_slab, gcn_slab, bcn_slab, gates)


def _merge_body(a_ref, c_ref, wo_ref, wpw_ref, sa_ref, sc_ref, o_ref):
    ya = _dot(a_ref[...], wo_ref[...])
    yc = _dot(c_ref[...], wpw_ref[...])
    o_ref[...] = (sa_ref[...].astype(F32) * ya + sc_ref[...].astype(F32) * yc).astype(BF16)


def _merge(a, cc, w_o, w_pw, layer, gates, col_a, col_c, tm, tn):
    t, mw = a.shape
    cw = cc.shape[1]
    d = w_o.shape[2]
    ja, jc = col_a // tn, col_c // tn
    return pl.pallas_call(
        _merge_body,
        grid=(t // tm, d // tn),
        in_specs=[pl.BlockSpec((tm, mw), lambda i, j: (i, 0)),
                  pl.BlockSpec((tm, cw), lambda i, j: (i, 0)),
                  pl.BlockSpec((None, mw, tn), lambda i, j: (layer, 0, j)),
                  pl.BlockSpec((None, cw, tn), lambda i, j: (layer, 0, j)),
                  pl.BlockSpec((tm, tn), lambda i, j: (i, ja + j)),
                  pl.BlockSpec((tm, tn), lambda i, j: (i, jc + j))],
        out_specs=pl.BlockSpec((tm, tn), lambda i, j: (i, j)),
        out_shape=jax.ShapeDtypeStruct((t, d), BF16),
        compiler_params=_params("parallel", "arbitrary"),
        name="merge_proj",
    )(a, cc, w_o, w_pw, gates, gates)


def _outproj_body(y_ref, w_ref, x_ref, o_ref):
    o_ref[...] = x_ref[...] + _dot(y_ref[...], w_ref[...])


def _outproj(y, w_out, layer, x, tm, tn):
    t, d = y.shape
    n = w_out.shape[2]
    return pl.pallas_call(
        _outproj_body,
        grid=(t // tm, n // tn),
        in_specs=[pl.BlockSpec((tm, d), lambda i, j: (i, 0)),
                  pl.BlockSpec((None, d, tn), lambda i, j: (layer, 0, j)),
                  pl.BlockSpec((tm, tn), lambda i, j: (i, j))],
        out_specs=pl.BlockSpec((tm, tn), lambda i, j: (i, j)),
        out_shape=jax.ShapeDtypeStruct((t, n), F32),
        compiler_params=_params("parallel", "arbitrary"),
        name="out_proj",
    )(y, w_out, x)


def _repack_body(w_ref, o_ref):
    o_ref[...] = w_ref[0].T.astype(BF16)


def _repack_tail(wt, row0, tr):
    depth, n, d = wt.shape
    return pl.pallas_call(
        _repack_body,
        grid=(depth, (n - row0) // tr),
        in_specs=[pl.BlockSpec((pl.Element(1), pl.Element(tr), pl.Element(d)),
                               lambda l, i: (l, pl.multiple_of(row0 + i * tr, math.gcd(row0, tr)), 0))],
        out_specs=pl.BlockSpec((None, d, tr), lambda l, i: (l, 0, i)),
        out_shape=jax.ShapeDtypeStruct((depth, d, n - row0), BF16),
        compiler_params=_params("parallel", "parallel"),
        name="w_in_repack",
    )(wt)


def _slab(p):
    l, r, c = p.shape
    return p.reshape(l, r, c // LANES, LANES).transpose(0, 2, 1, 3)


def _forward(x, positions, g_pre, w_in, g_q, w_q_up, g_kv, w_kv_up, w_o_mla, w_dw, b_dw, g_cn, b_cn,
             w_pw_out, w_out, g_final, *, n_heads):
    b, s, d = x.shape
    depth = g_pre.shape[0]
    t = b * s
    q_lora, kv_lora = g_q.shape[1], g_kv.shape[1]
    mw = n_heads * V_HEAD
    cw = w_dw.shape[2]
    off_kr = q_lora + kv_lora
    off_gmla = off_kr + QK_ROPE
    off_conv = off_gmla + mw
    off_gconv = off_conv + 2 * cw
    assert w_in.shape[2] == off_gconv + cw + 2 * d and cw == d

    tm = min(1024, s)
    tn = min(512, d)
    tn_gate = min(1024, d)
    tq = min(1024, s)
    tk = min(256, s)
    ts = min(256, s)
    heads_per_tile = min(8, n_heads)

    swap = jnp.concatenate([jnp.arange(QK_ROPE // 2, QK_ROPE), jnp.arange(0, QK_ROPE // 2)])
    w_kr = w_in[:, :, off_kr:off_gmla]
    w_lat = jnp.concatenate([w_in[:, :, :off_gmla], w_kr[:, :, swap]], axis=2).astype(BF16)
    w_tail = _repack_tail(jnp.swapaxes(w_in, 1, 2), off_gmla, min(512, d))
    wq = w_q_up.reshape(depth, q_lora, n_heads, QK_NOPE + QK_ROPE)
    w_q = jnp.concatenate([wq, wq[..., QK_NOPE:][..., swap]], axis=3).reshape(depth, q_lora, n_heads * HEAD_PAD)
    w_q = w_q.astype(BF16)
    w_kv = w_kv_up.astype(BF16)
    w_o = w_o_mla.astype(BF16)
    w_pw = w_pw_out.astype(BF16)
    w_o2 = w_out.astype(BF16)
    w_slab = _slab(jnp.pad(w_dw, ((0, 0), (0, CONV_TAPS_PAD - CONV_K), (0, 0))))
    bdw_slab, gcn_slab, bcn_slab = _slab(b_dw[:, None]), _slab(g_cn[:, None]), _slab(b_cn[:, None])

    inv_freq = 1.0 / (ROPE_THETA ** (jnp.arange(0, QK_ROPE, 2, dtype=F32) / QK_ROPE))
    cs = _rope_table(positions.reshape(t, 1), jnp.tile(inv_freq, 4)[None], min(1024, t))

    q_scale = (1.0 / math.sqrt(QK_NOPE + QK_ROPE)) * math.log2(math.e)
    x2 = x.reshape(t, d)
    h = _rmsnorm(x2, g_pre[0][None], BF16, min(256, t))
    for l in range(depth):
        cq, ckv, kr = _latent(h, w_lat, l, g_q[l][None], g_kv[l][None], cs, min(512, s))
        g_mla = _gates(h, w_tail, l, 0, mw, mw, tm, tn_gate)
        g_mix = _gates(h, w_tail, l, mw + 2 * cw, cw + 2 * d, cw, tm, tn_gate)
        u = _glu(h, w_tail, l, mw, cw, s, tm, tn)
        q = _qup(cq, w_q, l, cs, q_scale, tm, heads_per_tile)
        k, v = _kvup(ckv, w_kv, l, kr, tm, heads_per_tile)
        attn = _attention(q.reshape(b, s, -1), k.reshape(b, s, -1), v.reshape(b, s, -1), g_mla.reshape(b, s, mw),
                          n_heads, tq, tk)
        cc = _conv_module(u, w_slab, bdw_slab, gcn_slab, bcn_slab, g_mix.reshape(b, s, -1), 0, l, ts)
        y = _merge(attn.reshape(t, mw), cc.reshape(t, cw), w_o, w_pw, l, g_mix, cw, cw + d, tm, min(256, d))
        x2 = _outproj(y, w_o2, l, x2, tm, tn)
        if l + 1 < depth:
            h = _rmsnorm(x2, g_pre[l + 1][None], BF16, min(256, t))
    return _rmsnorm(x2, g_final[None], F32, min(256, t)).reshape(b, s, d)


def kernel(x, positions, g_pre, w_in, g_q, w_q_up, g_kv, w_kv_up, w_o_mla, w_dw, b_dw, g_cn, b_cn, w_pw_out, w_out, g_final):
    return _forward(x, positions, g_pre, w_in, g_q, w_q_up, g_kv, w_kv_up, w_o_mla, w_dw, b_dw, g_cn, b_cn,
                    w_pw_out, w_out, g_final, n_heads=N_HEADS)
```

```python
import functools
import math

import jax
import jax.numpy as jnp
from jax import lax
from jax.experimental import pallas as pl
from jax.experimental.pallas import tpu as pltpu

N_HEADS = 32
QK_NOPE = 128
QK_ROPE = 64
V_HEAD = 128
CONV_K = 31
EPS = 1e-6
ROPE_THETA = 10000.0

LANES = 128
HEAD_PAD = QK_NOPE + 2 * QK_ROPE
CONV_HALO = 32
CONV_TAPS_PAD = 32
GLU_K_PIECE = 256
ATTN_LOOKAHEAD = 2
VMEM_LIMIT_BYTES = 56 * 1024 * 1024

F32 = jnp.float32
BF16 = jnp.bfloat16

assert 2 * QK_ROPE == LANES and QK_NOPE == LANES and V_HEAD == LANES
assert CONV_HALO >= CONV_K - 1


def _params(*sem):
    return pltpu.CompilerParams(dimension_semantics=sem, vmem_limit_bytes=VMEM_LIMIT_BYTES)


def _dot(a, b):
    return jnp.dot(a, b, preferred_element_type=F32)


def _rope_mix(t):
    r = t + pltpu.roll(t, QK_ROPE, 1)
    lane = lax.broadcasted_iota(jnp.int32, r.shape, 1)
    return jnp.where(lane < QK_ROPE, r, 0.0)


def _rmsnorm_body(x_ref, g_ref, o_ref):
    x = x_ref[...]
    ms = jnp.mean(x * x, axis=-1, keepdims=True)
    o_ref[...] = (x * lax.rsqrt(ms + EPS) * g_ref[...]).astype(o_ref.dtype)


def _rmsnorm(x, g_row, out_dtype, tr):
    t, d = x.shape
    return pl.pallas_call(
        _rmsnorm_body,
        grid=(t // tr,),
        in_specs=[pl.BlockSpec((tr, d), lambda i: (i, 0)),
                  pl.BlockSpec((1, d), lambda i: (0, 0))],
        out_specs=pl.BlockSpec((tr, d), lambda i: (i, 0)),
        out_shape=jax.ShapeDtypeStruct((t, d), out_dtype),
        compiler_params=_params("parallel"),
        name="rmsnorm",
    )(x, g_row)


def _rope_table_body(pos_ref, inv_ref, o_ref):
    ang = pos_ref[...].astype(F32) * inv_ref[...]
    lane = lax.broadcasted_iota(jnp.int32, ang.shape, 1)
    c = jnp.cos(ang)
    s = jnp.sin(ang)
    o_ref[...] = jnp.where(lane < QK_ROPE, c, jnp.where(lane < QK_ROPE + QK_ROPE // 2, -s, s))


def _rope_table(pos_col, inv_row, tr):
    t = pos_col.shape[0]
    return pl.pallas_call(
        _rope_table_body,
        grid=(t // tr,),
        in_specs=[pl.BlockSpec((tr, 1), lambda i: (i, 0)),
                  pl.BlockSpec((1, LANES), lambda i: (0, 0))],
        out_specs=pl.BlockSpec((tr, LANES), lambda i: (i, 0)),
        out_shape=jax.ShapeDtypeStruct((t, LANES), F32),
        compiler_params=_params("parallel"),
        name="rope_table",
    )(pos_col, inv_row)


def _latent_body(h_ref, w_ref, gq_ref, gkv_ref, cs_ref, cq_ref, ckv_ref, kr_ref, *, q_lora, kv_lora):
    z = _dot(h_ref[...], w_ref[...])

    def rms(v, g):
        ms = jnp.mean(v * v, axis=-1, keepdims=True)
        return v * lax.rsqrt(ms + EPS) * g

    cq_ref[...] = rms(z[:, :q_lora], gq_ref[...]).astype(BF16)
    ckv_ref[...] = rms(z[:, q_lora:q_lora + kv_lora], gkv_ref[...]).astype(BF16)
    kr_ref[...] = _rope_mix(z[:, q_lora + kv_lora:] * cs_ref[...]).astype(BF16)


def _latent(h, w_lat, layer, gq_row, gkv_row, cs, tm):
    t, d = h.shape
    q_lora, kv_lora = gq_row.shape[1], gkv_row.shape[1]
    nl = w_lat.shape[2]
    return pl.pallas_call(
        functools.partial(_latent_body, q_lora=q_lora, kv_lora=kv_lora),
        grid=(t // tm,),
        in_specs=[pl.BlockSpec((tm, d), lambda i: (i, 0)),
                  pl.BlockSpec((None, d, nl), lambda i: (layer, 0, 0)),
                  pl.BlockSpec((1, q_lora), lambda i: (0, 0)),
                  pl.BlockSpec((1, kv_lora), lambda i: (0, 0)),
                  pl.BlockSpec((tm, LANES), lambda i: (i, 0))],
        out_specs=[pl.BlockSpec((tm, q_lora), lambda i: (i, 0)),
                   pl.BlockSpec((tm, kv_lora), lambda i: (i, 0)),
                   pl.BlockSpec((tm, LANES), lambda i: (i, 0))],
        out_shape=[jax.ShapeDtypeStruct((t, q_lora), BF16),
                   jax.ShapeDtypeStruct((t, kv_lora), BF16),
                   jax.ShapeDtypeStruct((t, LANES), BF16)],
        compiler_params=_params("parallel"),
        name="latent_proj",
    )(h, w_lat, gq_row, gkv_row, cs)


def _gates_body(h_ref, w_ref, o_ref, *, n_silu_tiles):
    z = _dot(h_ref[...], w_ref[...])
    s = jax.nn.sigmoid(z)
    is_silu = pl.program_id(1) < n_silu_tiles
    o_ref[...] = jnp.where(is_silu, z * s, s).astype(BF16)


def _gates(h, w, layer, col0, n, n_silu_cols, tm, tn):
    t, d = h.shape
    j0 = col0 // tn
    return pl.pallas_call(
        functools.partial(_gates_body, n_silu_tiles=n_silu_cols // tn),
        grid=(t // tm, n // tn),
        in_specs=[pl.BlockSpec((tm, d), lambda i, j: (i, 0)),
                  pl.BlockSpec((None, d, tn), lambda i, j: (layer, 0, j0 + j))],
        out_specs=pl.BlockSpec((tm, tn), lambda i, j: (i, j)),
        out_shape=jax.ShapeDtypeStruct((t, n), BF16),
        compiler_params=_params("parallel", "arbitrary"),
        name="gate_proj",
    )(h, w)


def _glu_conv_body(h_ref, wv_ref, wg_ref, wdw_ref, bdw_ref, o_ref, win0_ref, win1_ref, halo_ref,
                   *, nj, nsb, tm, rb, rc):
    i = pl.program_id(0)
    j = pl.program_id(1)
    ncb = o_ref.shape[0]
    base = CONV_HALO - (CONV_K - 1)
    wins = (win0_ref, win1_ref)

    nrb = tm // rb
    seq_start = (i % nsb) == 0

    kp = min(GLU_K_PIECE, h_ref.shape[1])
    n_kp = h_ref.shape[1] // kp

    def dots(b, kk):
        h = h_ref[b * rb:(b + 1) * rb, kk * kp:(kk + 1) * kp]
        return _dot(h, wv_ref[kk * kp:(kk + 1) * kp, :]), _dot(h, wg_ref[kk * kp:(kk + 1) * kp, :])

    def store_u(win_ref, b, val, gate):
        u = val * jax.nn.sigmoid(gate)
        for c in range(ncb):
            uc = u[:, c * LANES:(c + 1) * LANES]
            slab = j * ncb + c
            if b == 0:
                win_ref[c, 0:CONV_HALO, :] = jnp.where(seq_start, 0.0, halo_ref[slab])
            win_ref[c, CONV_HALO + b * rb:CONV_HALO + (b + 1) * rb, :] = uc
            if b == nrb - 1:
                halo_ref[slab] = uc[rb - CONV_HALO:, :]

    def conv_piece(win_ref, c, r0):
        acc = jnp.broadcast_to(bdw_ref[c], (rc, LANES))
        for k in range(CONV_K):
            acc = acc + win_ref[c, base + r0 + k:base + r0 + k + rc, :] * wdw_ref[c, k:k + 1, :]
        o_ref[c, r0:r0 + rc, :] = acc

    def region(project_win, conv_win):
        pieces = [(c, r0) for c in range(ncb) for r0 in range(0, tm, rc)]
        steps = [(b, kk) for b in range(nrb) for kk in range(n_kp)]
        share = -(-len(pieces) // len(steps))
        val = gate = None
        for n, (b, kk) in enumerate(steps):
            if project_win is not None:
                v, g = dots(b, kk)
            if conv_win is not None:
                for c, r0 in pieces[n * share:(n + 1) * share]:
                    conv_piece(conv_win, c, r0)
            if project_win is not None:
                val, gate = (v, g) if kk == 0 else (val + v, gate + g)
                if kk == n_kp - 1:
                    store_u(project_win, b, val, gate)

    @pl.when((i == 0) & (j == 0))
    def _():
        halo_ref[...] = jnp.zeros(halo_ref.shape, F32)

    @pl.when(j == 0)
    def _():
        region(wins[0], None)

    for p in range(2):
        @pl.when((j % 2 == p) & (j > 0) & (j < nj))
        def _(p=p):
            region(wins[p], wins[1 - p])

    @pl.when(j == nj)
    def _():
        region(None, wins[(nj - 1) % 2])


def _glu_conv(h, w, layer, col0, cw, w_slab, bdw_slab, seq, tm, tn):
    t, d = h.shape
    nsb = seq // tm
    nj = cw // tn
    j0 = col0 // tn
    ncb = tn // LANES
    assert nj % 2 == 0 and tm >= CONV_HALO
    last = nj - 1
    return pl.pallas_call(
        functools.partial(_glu_conv_body, nj=nj, nsb=nsb, tm=tm, rb=min(256, tm), rc=min(64, tm)),
        grid=(t // tm, nj + 1),
        in_specs=[pl.BlockSpec((tm, d), lambda i, j: (i, 0)),
                  pl.BlockSpec((None, d, tn), lambda i, j: (layer, 0, j0 + jnp.minimum(j, last))),
                  pl.BlockSpec((None, d, tn), lambda i, j: (layer, 0, j0 + nj + jnp.minimum(j, last))),
                  pl.BlockSpec((None, ncb, CONV_TAPS_PAD, LANES), lambda i, j: (layer, jnp.maximum(j - 1, 0), 0, 0)),
                  pl.BlockSpec((None, ncb, 1, LANES), lambda i, j: (layer, jnp.maximum(j - 1, 0), 0, 0))],
        out_specs=pl.BlockSpec((None, ncb, tm, LANES),
                               lambda i, j: (i // nsb, jnp.maximum(j - 1, 0), i % nsb, 0)),
        out_shape=jax.ShapeDtypeStruct((t // seq, cw // LANES, seq, LANES), F32),
        scratch_shapes=[pltpu.VMEM((ncb, CONV_HALO + tm, LANES), F32),
                        pltpu.VMEM((ncb, CONV_HALO + tm, LANES), F32),
                        pltpu.VMEM((cw // LANES, CONV_HALO, LANES), F32)],
        compiler_params=_params("arbitrary", "arbitrary"),
        name="glu_conv",
    )(h, w, w, w_slab, bdw_slab)


def _qup_body(a_ref, w_ref, cs_ref, o_ref, *, heads, scale):
    y = _dot(a_ref[...], w_ref[...])
    cs = cs_ref[...]
    for i in range(heads):
        c0 = i * HEAD_PAD
        o_ref[:, c0:c0 + QK_NOPE] = (y[:, c0:c0 + QK_NOPE] * scale).astype(BF16)
        o_ref[:, c0 + QK_NOPE:c0 + HEAD_PAD] = (_rope_mix(y[:, c0 + QK_NOPE:c0 + HEAD_PAD] * cs) * scale).astype(BF16)


def _qup(cq, w_q, layer, cs, scale, tm, heads):
    t, ql = cq.shape
    n = w_q.shape[2]
    tn = heads * HEAD_PAD
    return pl.pallas_call(
        functools.partial(_qup_body, heads=heads, scale=scale),
        grid=(t // tm, n // tn),
        in_specs=[pl.BlockSpec((tm, ql), lambda i, j: (i, 0)),
                  pl.BlockSpec((None, ql, tn), lambda i, j: (layer, 0, j)),
                  pl.BlockSpec((tm, LANES), lambda i, j: (i, 0))],
        out_specs=pl.BlockSpec((tm, tn), lambda i, j: (i, j)),
        out_shape=jax.ShapeDtypeStruct((t, n), BF16),
        compiler_params=_params("parallel", "arbitrary"),
        name="q_up",
    )(cq, w_q, cs)


def _kvup_body(a_ref, w_ref, kr_ref, k_ref, v_ref, *, heads):
    y = _dot(a_ref[...], w_ref[...])
    kr = kr_ref[...]
    for i in range(heads):
        c0 = i * HEAD_PAD
        k_ref[:, c0:c0 + QK_NOPE] = y[:, c0:c0 + QK_NOPE].astype(BF16)
        k_ref[:, c0 + QK_NOPE:c0 + HEAD_PAD] = kr
        v_ref[:, i * V_HEAD:(i + 1) * V_HEAD] = y[:, c0 + QK_NOPE:c0 + QK_NOPE + V_HEAD].astype(BF16)


def _kvup(ckv, w_kv, layer, kr, tm, heads):
    t, kvl = ckv.shape
    n_heads = w_kv.shape[2] // (QK_NOPE + V_HEAD)
    tn = heads * (QK_NOPE + V_HEAD)
    return pl.pallas_call(
        functools.partial(_kvup_body, heads=heads),
        grid=(t // tm, n_heads // heads),
        in_specs=[pl.BlockSpec((tm, kvl), lambda i, j: (i, 0)),
                  pl.BlockSpec((None, kvl, tn), lambda i, j: (layer, 0, j)),
                  pl.BlockSpec((tm, LANES), lambda i, j: (i, 0))],
        out_specs=[pl.BlockSpec((tm, heads * HEAD_PAD), lambda i, j: (i, j)),
                   pl.BlockSpec((tm, heads * V_HEAD), lambda i, j: (i, j))],
        out_shape=[jax.ShapeDtypeStruct((t, n_heads * HEAD_PAD), BF16),
                   jax.ShapeDtypeStruct((t, n_heads * V_HEAD), BF16)],
        compiler_params=_params("parallel", "arbitrary"),
        name="kv_up",
    )(ckv, w_kv, kr)


def _attn_body(q_ref, k_ref, v_ref, g_ref, o_ref, vt_ref, qt_ref, *, tq, tk):
    qi = pl.program_id(2)
    n_kv = v_ref.shape[0] // tk
    cpq = tq // tk

    @pl.when(qi == 0)
    def _():
        for c in range(n_kv):
            vt_ref[c] = v_ref[c * tk:(c + 1) * tk, :].T

    qt_ref[...] = q_ref[...].T

    def scores(c, q0):
        return _dot(k_ref[c * tk:(c + 1) * tk, :], qt_ref[:, q0:])

    def update(c, st, state, q0, masked):
        m, l, acc = (x[:, q0:] for x in state)
        if masked:
            key = lax.broadcasted_iota(jnp.int32, st.shape, 0)
            qry = lax.broadcasted_iota(jnp.int32, st.shape, 1)
            st = jnp.where(key <= qry, st, -1e30)
        m_new = jnp.maximum(m, jnp.max(st, axis=0, keepdims=True))
        alpha = jnp.exp2(m - m_new)
        pt = jnp.exp2(st - m_new)
        l = alpha * l + jnp.sum(pt, axis=0, keepdims=True)
        acc = alpha * acc + _dot(vt_ref[c], pt.astype(BF16))
        if q0 == 0:
            return m_new, l, acc
        return tuple(jnp.concatenate([old[:, :q0], new], axis=1) for old, new in zip(state, (m_new, l, acc)))

    def q_tile_program(t):
        n_chunks = (t + 1) * cpq
        q0_of = lambda c: max(c - t * cpq, 0) * tk
        sts = {c: scores(c, q0_of(c)) for c in range(min(ATTN_LOOKAHEAD, n_chunks))}
        state = (jnp.full((1, tq), -1e30, F32), jnp.zeros((1, tq), F32), jnp.zeros((V_HEAD, tq), F32))
        for c in range(n_chunks):
            state = update(c, sts.pop(c), state, q0_of(c), c >= t * cpq)
            if c + ATTN_LOOKAHEAD < n_chunks:
                sts[c + ATTN_LOOKAHEAD] = scores(c + ATTN_LOOKAHEAD, q0_of(c + ATTN_LOOKAHEAD))
        _, l, acc = state
        out = (acc * (1.0 / l)).T
        o_ref[...] = (out * g_ref[...].astype(F32)).astype(BF16)

    for t in range(n_kv // cpq):
        pl.when(qi == t)(functools.partial(q_tile_program, t))


def _attention(q, k, v, gates, n_heads, tq, tk):
    b, s, _ = q.shape
    return pl.pallas_call(
        functools.partial(_attn_body, tq=tq, tk=tk),
        grid=(b, n_heads, s // tq),
        scratch_shapes=[pltpu.VMEM((s // tk, V_HEAD, tk), BF16),
                        pltpu.VMEM((HEAD_PAD, tq), BF16)],
        in_specs=[pl.BlockSpec((None, tq, HEAD_PAD), lambda bi, h, i: (bi, i, h)),
                  pl.BlockSpec((None, s, HEAD_PAD), lambda bi, h, i: (bi, 0, h)),
                  pl.BlockSpec((None, s, V_HEAD), lambda bi, h, i: (bi, 0, h)),
                  pl.BlockSpec((None, tq, V_HEAD), lambda bi, h, i: (bi, i, h))],
        out_specs=pl.BlockSpec((None, tq, V_HEAD), lambda bi, h, i: (bi, i, h)),
        out_shape=jax.ShapeDtypeStruct((b, s, n_heads * V_HEAD), BF16),
        compiler_params=_params("parallel", "parallel", "arbitrary"),
        name="mla_attention",
    )(q, k, v, gates)


def _conv_finish_body(y_ref, gcn_ref, bcn_ref, gate_ref, o_ref, *, ts, rf):
    nc = y_ref.shape[0]
    inv_c = 1.0 / (nc * LANES)
    for r0 in range(0, ts, rf):
        tot = y_ref[0, r0:r0 + rf, :]
        for c in range(1, nc):
            tot = tot + y_ref[c, r0:r0 + rf, :]
        mu = jnp.broadcast_to(jnp.sum(tot, axis=-1, keepdims=True) * inv_c, (rf, LANES))
        sq = None
        for c in range(nc):
            d = y_ref[c, r0:r0 + rf, :] - mu
            sq = d * d if sq is None else sq + d * d
        rstd = jnp.broadcast_to(lax.rsqrt(jnp.sum(sq, axis=-1, keepdims=True) * inv_c + EPS), (rf, LANES))
        for c in range(nc):
            z = (y_ref[c, r0:r0 + rf, :] - mu) * rstd * gcn_ref[c] + bcn_ref[c]
            z = z * jax.nn.sigmoid(z)
            g = gate_ref[r0:r0 + rf, c * LANES:(c + 1) * LANES].astype(F32)
            o_ref[r0:r0 + rf, c * LANES:(c + 1) * LANES] = (z * g).astype(BF16)


def _conv_finish(y, gcn_slab, bcn_slab, gates, gate_col0, layer, ts):
    b, nc, s, _ = y.shape
    c = nc * LANES
    gblk = gate_col0 // c
    return pl.pallas_call(
        functools.partial(_conv_finish_body, ts=ts, rf=min(64, ts)),
        grid=(b, s // ts),
        in_specs=[pl.BlockSpec((None, nc, ts, LANES), lambda bi, i: (bi, 0, i, 0)),
                  pl.BlockSpec((None, nc, 1, LANES), lambda bi, i: (layer, 0, 0, 0)),
                  pl.BlockSpec((None, nc, 1, LANES), lambda bi, i: (layer, 0, 0, 0)),
                  pl.BlockSpec((None, ts, c), lambda bi, i: (bi, i, gblk))],
        out_specs=pl.BlockSpec((None, ts, c), lambda bi, i: (bi, i, 0)),
        out_shape=jax.ShapeDtypeStruct((b, s, c), BF16),
        compiler_params=_params("parallel", "parallel"),
        name="conv_finish",
    )(y, gcn_slab, bcn_slab, gates)


def _merge_body(a_ref, c_ref, wo_ref, wpw_ref, sa_ref, sc_ref, o_ref):
    ya = _dot(a_ref[...], wo_ref[...])
    yc = _dot(c_ref[...], wpw_ref[...])
    o_ref[...] = (sa_ref[...].astype(F32) * ya + sc_ref[...].astype(F32) * yc).astype(BF16)


def _merge(a, cc, w_o, w_pw, layer, gates, col_a, col_c, tm, tn):
    t, mw = a.shape
    cw = cc.shape[1]
    d = w_o.shape[2]
    ja, jc = col_a // tn, col_c // tn
    return pl.pallas_call(
        _merge_body,
        grid=(t // tm, d // tn),
        in_specs=[pl.BlockSpec((tm, mw), lambda i, j: (i, 0)),
                  pl.BlockSpec((tm, cw), lambda i, j: (i, 0)),
                  pl.BlockSpec((None, mw, tn), lambda i, j: (layer, 0, j)),
                  pl.BlockSpec((None, cw, tn), lambda i, j: (layer, 0, j)),
                  pl.BlockSpec((tm, tn), lambda i, j: (i, ja + j)),
                  pl.BlockSpec((tm, tn), lambda i, j: (i, jc + j))],
        out_specs=pl.BlockSpec((tm, tn), lambda i, j: (i, j)),
        out_shape=jax.ShapeDtypeStruct((t, d), BF16),
        compiler_params=_params("parallel", "arbitrary"),
        name="merge_proj",
    )(a, cc, w_o, w_pw, gates, gates)


def _outproj_body(y_ref, w_ref, x_ref, o_ref):
    o_ref[...] = x_ref[...] + _dot(y_ref[...], w_ref[...])


def _outproj(y, w_out, layer, x, tm, tn):
    t, d = y.shape
    n = w_out.shape[2]
    return pl.pallas_call(
        _outproj_body,
        grid=(t // tm, n // tn),
        in_specs=[pl.BlockSpec((tm, d), lambda i, j: (i, 0)),
                  pl.BlockSpec((None, d, tn), lambda i, j: (layer, 0, j)),
                  pl.BlockSpec((tm, tn), lambda i, j: (i, j))],
        out_specs=pl.BlockSpec((tm, tn), lambda i, j: (i, j)),
        out_shape=jax.ShapeDtypeStruct((t, n), F32),
        compiler_params=_params("parallel", "arbitrary"),
        name="out_proj",
    )(y, w_out, x)


def _repack_body(w_ref, o_ref):
    o_ref[...] = w_ref[0].T.astype(BF16)


def _repack_tail(wt, row0, tr):
    depth, n, d = wt.shape
    return pl.pallas_call(
        _repack_body,
        grid=(depth, (n - row0) // tr),
        in_specs=[pl.BlockSpec((pl.Element(1), pl.Element(tr), pl.Element(d)),
                               lambda l, i: (l, pl.multiple_of(row0 + i * tr, math.gcd(row0, tr)), 0))],
        out_specs=pl.BlockSpec((None, d, tr), lambda l, i: (l, 0, i)),
        out_shape=jax.ShapeDtypeStruct((depth, d, n - row0), BF16),
        compiler_params=_params("parallel", "parallel"),
        name="w_in_repack",
    )(wt)


def _slab(p):
    l, r, c = p.shape
    return p.reshape(l, r, c // LANES, LANES).transpose(0, 2, 1, 3)


def _forward(x, positions, g_pre, w_in, g_q, w_q_up, g_kv, w_kv_up, w_o_mla, w_dw, b_dw, g_cn, b_cn,
             w_pw_out, w_out, g_final, *, n_heads):
    b, s, d = x.shape
    depth = g_pre.shape[0]
    t = b * s
    q_lora, kv_lora = g_q.shape[1], g_kv.shape[1]
    mw = n_heads * V_HEAD
    cw = w_dw.shape[2]
    off_kr = q_lora + kv_lora
    off_gmla = off_kr + QK_ROPE
    off_conv = off_gmla + mw
    off_gconv = off_conv + 2 * cw
    assert w_in.shape[2] == off_gconv + cw + 2 * d and cw == d

    tm = min(1024, s)
    tn = min(512, d)
    tn_gate = min(1024, d)
    tn_glu = min(512, d // 4)
    tq = min(1024, s)
    tk = min(256, s)
    ts = min(256, s)
    heads_per_tile = min(8, n_heads)

    swap = jnp.concatenate([jnp.arange(QK_ROPE // 2, QK_ROPE), jnp.arange(0, QK_ROPE // 2)])
    w_kr = w_in[:, :, off_kr:off_gmla]
    w_lat = jnp.concatenate([w_in[:, :, :off_gmla], w_kr[:, :, swap]], axis=2).astype(BF16)
    w_tail = _repack_tail(jnp.swapaxes(w_in, 1, 2), off_gmla, min(512, d))
    wq = w_q_up.reshape(depth, q_lora, n_heads, QK_NOPE + QK_ROPE)
    w_q = jnp.concatenate([wq, wq[..., QK_NOPE:][..., swap]], axis=3).reshape(depth, q_lora, n_heads * HEAD_PAD)
    w_q = w_q.astype(BF16)
    w_kv = w_kv_up.astype(BF16)
    w_o = w_o_mla.astype(BF16)
    w_pw = w_pw_out.astype(BF16)
    w_o2 = w_out.astype(BF16)
    w_slab = _slab(jnp.pad(w_dw, ((0, 0), (0, CONV_TAPS_PAD - CONV_K), (0, 0))))
    bdw_slab, gcn_slab, bcn_slab = _slab(b_dw[:, None]), _slab(g_cn[:, None]), _slab(b_cn[:, None])

    inv_freq = 1.0 / (ROPE_THETA ** (jnp.arange(0, QK_ROPE, 2, dtype=F32) / QK_ROPE))
    cs = _rope_table(positions.reshape(t, 1), jnp.tile(inv_freq, 4)[None], min(1024, t))

    q_scale = (1.0 / math.sqrt(QK_NOPE + QK_ROPE)) * math.log2(math.e)
    x2 = x.reshape(t, d)
    h = _rmsnorm(x2, g_pre[0][None], BF16, min(256, t))
    for l in range(depth):
        cq, ckv, kr = _latent(h, w_lat, l, g_q[l][None], g_kv[l][None], cs, min(512, s))
        g_mla = _gates(h, w_tail, l, 0, mw, mw, tm, tn_gate)
        g_mix = _gates(h, w_tail, l, mw + 2 * cw, cw + 2 * d, cw, tm, tn_gate)
        y_conv = _glu_conv(h, w_tail, l, mw, cw, w_slab, bdw_slab, s, tm, tn_glu)
        q = _qup(cq, w_q, l, cs, q_scale, tm, heads_per_tile)
        k, v = _kvup(ckv, w_kv, l, kr, tm, heads_per_tile)
        attn = _attention(q.reshape(b, s, -1), k.reshape(b, s, -1), v.reshape(b, s, -1), g_mla.reshape(b, s, mw),
                          n_heads, tq, tk)
        cc = _conv_finish(y_conv, gcn_slab, bcn_slab, g_mix.reshape(b, s, -1), 0, l, ts)
        y = _merge(attn.reshape(t, mw), cc.reshape(t, cw), w_o, w_pw, l, g_mix, cw, cw + d, tm, min(256, d))
        x2 = _outproj(y, w_o2, l, x2, tm, tn)
        if l + 1 < depth:
            h = _rmsnorm(x2, g_pre[l + 1][None], BF16, min(256, t))
    return _rmsnorm(x2, g_final[None], F32, min(256, t)).reshape(b, s, d)


def kernel(x, positions, g_pre, w_in, g_q, w_q_up, g_kv, w_kv_up, w_o_mla, w_dw, b_dw, g_cn, b_cn, w_pw_out, w_out, g_final):
    return _forward(x, positions, g_pre, w_in, g_q, w_q_up, g_kv, w_kv_up, w_o_mla, w_dw, b_dw, g_cn, b_cn,
                    w_pw_out, w_out, g_final, n_heads=N_HEADS)
```

```python
import functools
import math

import jax
import jax.numpy as jnp
from jax import lax
from jax.experimental import pallas as pl
from jax.experimental.pallas import tpu as pltpu

N_HEADS = 32
QK_NOPE = 128
QK_ROPE = 64
V_HEAD = 128
CONV_K = 31
EPS = 1e-6
ROPE_THETA = 10000.0

LANES = 128
HEAD_PAD = QK_NOPE + 2 * QK_ROPE
CONV_HALO = 32
CONV_TAPS_PAD = 32
GLU_K_PIECE = 256
ATTN_LOOKAHEAD = 2
VMEM_LIMIT_BYTES = 56 * 1024 * 1024

F32 = jnp.float32
BF16 = jnp.bfloat16

assert 2 * QK_ROPE == LANES and QK_NOPE == LANES and V_HEAD == LANES
assert CONV_HALO >= CONV_K - 1


def _params(*sem):
    return pltpu.CompilerParams(dimension_semantics=sem, vmem_limit_bytes=VMEM_LIMIT_BYTES)


def _dot(a, b):
    return jnp.dot(a, b, preferred_element_type=F32)


def _rope_mix(t):
    r = t + pltpu.roll(t, QK_ROPE, 1)
    lane = lax.broadcasted_iota(jnp.int32, r.shape, 1)
    return jnp.where(lane < QK_ROPE, r, 0.0)


def _rmsnorm_body(x_ref, g_ref, o_ref):
    x = x_ref[...]
    ms = jnp.mean(x * x, axis=-1, keepdims=True)
    o_ref[...] = (x * lax.rsqrt(ms + EPS) * g_ref[...]).astype(o_ref.dtype)


def _rmsnorm(x, g_row, out_dtype, tr):
    t, d = x.shape
    return pl.pallas_call(
        _rmsnorm_body,
        grid=(t // tr,),
        in_specs=[pl.BlockSpec((tr, d), lambda i: (i, 0)),
                  pl.BlockSpec((1, d), lambda i: (0, 0))],
        out_specs=pl.BlockSpec((tr, d), lambda i: (i, 0)),
        out_shape=jax.ShapeDtypeStruct((t, d), out_dtype),
        compiler_params=_params("parallel"),
        name="rmsnorm",
    )(x, g_row)


def _rope_table_body(pos_ref, inv_ref, o_ref):
    ang = pos_ref[...].astype(F32) * inv_ref[...]
    lane = lax.broadcasted_iota(jnp.int32, ang.shape, 1)
    c = jnp.cos(ang)
    s = jnp.sin(ang)
    o_ref[...] = jnp.where(lane < QK_ROPE, c, jnp.where(lane < QK_ROPE + QK_ROPE // 2, -s, s))


def _rope_table(pos_col, inv_row, tr):
    t = pos_col.shape[0]
    return pl.pallas_call(
        _rope_table_body,
        grid=(t // tr,),
        in_specs=[pl.BlockSpec((tr, 1), lambda i: (i, 0)),
                  pl.BlockSpec((1, LANES), lambda i: (0, 0))],
        out_specs=pl.BlockSpec((tr, LANES), lambda i: (i, 0)),
        out_shape=jax.ShapeDtypeStruct((t, LANES), F32),
        compiler_params=_params("parallel"),
        name="rope_table",
    )(pos_col, inv_row)


def _latent_body(h_ref, w_ref, gq_ref, gkv_ref, cs_ref, cq_ref, ckv_ref, kr_ref, *, q_lora, kv_lora):
    z = _dot(h_ref[...], w_ref[...])

    def rms(v, g):
        ms = jnp.mean(v * v, axis=-1, keepdims=True)
        return v * lax.rsqrt(ms + EPS) * g

    cq_ref[...] = rms(z[:, :q_lora], gq_ref[...]).astype(BF16)
    ckv_ref[...] = rms(z[:, q_lora:q_lora + kv_lora], gkv_ref[...]).astype(BF16)
    kr_ref[...] = _rope_mix(z[:, q_lora + kv_lora:] * cs_ref[...]).astype(BF16)


def _latent(h, w_lat, layer, gq_row, gkv_row, cs, tm):
    t, d = h.shape
    q_lora, kv_lora = gq_row.shape[1], gkv_row.shape[1]
    nl = w_lat.shape[2]
    return pl.pallas_call(
        functools.partial(_latent_body, q_lora=q_lora, kv_lora=kv_lora),
        grid=(t // tm,),
        in_specs=[pl.BlockSpec((tm, d), lambda i: (i, 0)),
                  pl.BlockSpec((None, d, nl), lambda i: (layer, 0, 0)),
                  pl.BlockSpec((1, q_lora), lambda i: (0, 0)),
                  pl.BlockSpec((1, kv_lora), lambda i: (0, 0)),
                  pl.BlockSpec((tm, LANES), lambda i: (i, 0))],
        out_specs=[pl.BlockSpec((tm, q_lora), lambda i: (i, 0)),
                   pl.BlockSpec((tm, kv_lora), lambda i: (i, 0)),
                   pl.BlockSpec((tm, LANES), lambda i: (i, 0))],
        out_shape=[jax.ShapeDtypeStruct((t, q_lora), BF16),
                   jax.ShapeDtypeStruct((t, kv_lora), BF16),
                   jax.ShapeDtypeStruct((t, LANES), BF16)],
        compiler_params=_params("parallel"),
        name="latent_proj",
    )(h, w_lat, gq_row, gkv_row, cs)


def _gates_body(h_ref, w_ref, o_ref, *, n_silu_tiles):
    z = _dot(h_ref[...], w_ref[...])
    s = jax.nn.sigmoid(z)
    is_silu = pl.program_id(1) < n_silu_tiles
    o_ref[...] = jnp.where(is_silu, z * s, s).astype(BF16)


def _gates(h, w, layer, col0, n, n_silu_cols, tm, tn):
    t, d = h.shape
    j0 = col0 // tn
    return pl.pallas_call(
        functools.partial(_gates_body, n_silu_tiles=n_silu_cols // tn),
        grid=(t // tm, n // tn),
        in_specs=[pl.BlockSpec((tm, d), lambda i, j: (i, 0)),
                  pl.BlockSpec((None, d, tn), lambda i, j: (layer, 0, j0 + j))],
        out_specs=pl.BlockSpec((tm, tn), lambda i, j: (i, j)),
        out_shape=jax.ShapeDtypeStruct((t, n), BF16),
        compiler_params=_params("parallel", "arbitrary"),
        name="gate_proj",
    )(h, w)


def _glu_conv_body(h_ref, wv_ref, wg_ref, wdw_ref, bdw_ref, o_ref, win0_ref, win1_ref, halo_ref,
                   *, nj, nsb, tm, rb, rc):
    i = pl.program_id(0)
    j = pl.program_id(1)
    ncb = o_ref.shape[0]
    base = CONV_HALO - (CONV_K - 1)
    wins = (win0_ref, win1_ref)

    nrb = tm // rb
    seq_start = (i % nsb) == 0

    kp = min(GLU_K_PIECE, h_ref.shape[1])
    n_kp = h_ref.shape[1] // kp

    def dots(b, kk):
        h = h_ref[b * rb:(b + 1) * rb, kk * kp:(kk + 1) * kp]
        return _dot(h, wv_ref[kk * kp:(kk + 1) * kp, :]), _dot(h, wg_ref[kk * kp:(kk + 1) * kp, :])

    def store_u(win_ref, b, val, gate):
        u = val * jax.nn.sigmoid(gate)
        for c in range(ncb):
            uc = u[:, c * LANES:(c + 1) * LANES]
            slab = j * ncb + c
            if b == 0:
                win_ref[c, 0:CONV_HALO, :] = jnp.where(seq_start, 0.0, halo_ref[slab])
            win_ref[c, CONV_HALO + b * rb:CONV_HALO + (b + 1) * rb, :] = uc
            if b == nrb - 1:
                halo_ref[slab] = uc[rb - CONV_HALO:, :]

    def conv_piece(win_ref, c, r0):
        acc = jnp.broadcast_to(bdw_ref[c], (rc, LANES))
        for k in range(CONV_K):
            acc = acc + win_ref[c, base + r0 + k:base + r0 + k + rc, :] * wdw_ref[c, k:k + 1, :]
        o_ref[c, r0:r0 + rc, :] = acc

    def region(project_win, conv_win):
        pieces = [(c, r0) for c in range(ncb) for r0 in range(0, tm, rc)]
        steps = [(b, kk) for b in range(nrb) for kk in range(n_kp)]
        share = -(-len(pieces) // len(steps))
        val = gate = None
        for n, (b, kk) in enumerate(steps):
            if project_win is not None:
                v, g = dots(b, kk)
            if conv_win is not None:
                for c, r0 in pieces[n * share:(n + 1) * share]:
                    conv_piece(conv_win, c, r0)
            if project_win is not None:
                val, gate = (v, g) if kk == 0 else (val + v, gate + g)
                if kk == n_kp - 1:
                    store_u(project_win, b, val, gate)

    @pl.when((i == 0) & (j == 0))
    def _():
        halo_ref[...] = jnp.zeros(halo_ref.shape, F32)

    @pl.when(j == 0)
    def _():
        region(wins[0], None)

    for p in range(2):
        @pl.when((j % 2 == p) & (j > 0) & (j < nj))
        def _(p=p):
            region(wins[p], wins[1 - p])

    @pl.when(j == nj)
    def _():
        region(None, wins[(nj - 1) % 2])


def _glu_conv(h, w, layer, col0, cw, w_slab, bdw_slab, seq, tm, tn):
    t, d = h.shape
    nsb = seq // tm
    nj = cw // tn
    j0 = col0 // tn
    ncb = tn // LANES
    assert nj % 2 == 0 and tm >= CONV_HALO
    last = nj - 1
    return pl.pallas_call(
        functools.partial(_glu_conv_body, nj=nj, nsb=nsb, tm=tm, rb=min(256, tm), rc=min(64, tm)),
        grid=(t // tm, nj + 1),
        in_specs=[pl.BlockSpec((tm, d), lambda i, j: (i, 0)),
                  pl.BlockSpec((None, d, tn), lambda i, j: (layer, 0, j0 + jnp.minimum(j, last))),
                  pl.BlockSpec((None, d, tn), lambda i, j: (layer, 0, j0 + nj + jnp.minimum(j, last))),
                  pl.BlockSpec((None, ncb, CONV_TAPS_PAD, LANES), lambda i, j: (layer, jnp.maximum(j - 1, 0), 0, 0)),
                  pl.BlockSpec((None, ncb, 1, LANES), lambda i, j: (layer, jnp.maximum(j - 1, 0), 0, 0))],
        out_specs=pl.BlockSpec((None, ncb, tm, LANES),
                               lambda i, j: (i // nsb, jnp.maximum(j - 1, 0), i % nsb, 0)),
        out_shape=jax.ShapeDtypeStruct((t // seq, cw // LANES, seq, LANES), F32),
        scratch_shapes=[pltpu.VMEM((ncb, CONV_HALO + tm, LANES), F32),
                        pltpu.VMEM((ncb, CONV_HALO + tm, LANES), F32),
                        pltpu.VMEM((cw // LANES, CONV_HALO, LANES), F32)],
        compiler_params=_params("arbitrary", "arbitrary"),
        name="glu_conv",
    )(h, w, w, w_slab, bdw_slab)


def _qup_body(a_ref, w_ref, cs_ref, o_ref, *, heads, scale):
    y = _dot(a_ref[...], w_ref[...])
    cs = cs_ref[...]
    for i in range(heads):
        c0 = i * HEAD_PAD
        o_ref[:, c0:c0 + QK_NOPE] = (y[:, c0:c0 + QK_NOPE] * scale).astype(BF16)
        o_ref[:, c0 + QK_NOPE:c0 + HEAD_PAD] = (_rope_mix(y[:, c0 + QK_NOPE:c0 + HEAD_PAD] * cs) * scale).astype(BF16)


def _qup(cq, w_q, layer, cs, scale, tm, heads):
    t, ql = cq.shape
    n = w_q.shape[2]
    tn = heads * HEAD_PAD
    return pl.pallas_call(
        functools.partial(_qup_body, heads=heads, scale=scale),
        grid=(t // tm, n // tn),
        in_specs=[pl.BlockSpec((tm, ql), lambda i, j: (i, 0)),
                  pl.BlockSpec((None, ql, tn), lambda i, j: (layer, 0, j)),
                  pl.BlockSpec((tm, LANES), lambda i, j: (i, 0))],
        out_specs=pl.BlockSpec((tm, tn), lambda i, j: (i, j)),
        out_shape=jax.ShapeDtypeStruct((t, n), BF16),
        compiler_params=_params("parallel", "arbitrary"),
        name="q_up",
    )(cq, w_q, cs)


def _kvup_body(a_ref, w_ref, kr_ref, k_ref, v_ref, *, heads):
    y = _dot(a_ref[...], w_ref[...])
    kr = kr_ref[...]
    for i in range(heads):
        c0 = i * HEAD_PAD
        k_ref[:, c0:c0 + QK_NOPE] = y[:, c0:c0 + QK_NOPE].astype(BF16)
        k_ref[:, c0 + QK_NOPE:c0 + HEAD_PAD] = kr
        v_ref[:, i * V_HEAD:(i + 1) * V_HEAD] = y[:, c0 + QK_NOPE:c0 + QK_NOPE + V_HEAD].astype(BF16)


def _kvup(ckv, w_kv, layer, kr, tm, heads):
    t, kvl = ckv.shape
    n_heads = w_kv.shape[2] // (QK_NOPE + V_HEAD)
    tn = heads * (QK_NOPE + V_HEAD)
    return pl.pallas_call(
        functools.partial(_kvup_body, heads=heads),
        grid=(t // tm, n_heads // heads),
        in_specs=[pl.BlockSpec((tm, kvl), lambda i, j: (i, 0)),
                  pl.BlockSpec((None, kvl, tn), lambda i, j: (layer, 0, j)),
                  pl.BlockSpec((tm, LANES), lambda i, j: (i, 0))],
        out_specs=[pl.BlockSpec((tm, heads * HEAD_PAD), lambda i, j: (i, j)),
                   pl.BlockSpec((tm, heads * V_HEAD), lambda i, j: (i, j))],
        out_shape=[jax.ShapeDtypeStruct((t, n_heads * HEAD_PAD), BF16),
                   jax.ShapeDtypeStruct((t, n_heads * V_HEAD), BF16)],
        compiler_params=_params("parallel", "arbitrary"),
        name="kv_up",
    )(ckv, w_kv, kr)


def _attn_body(q_ref, k_ref, v_ref, g_ref, o_ref, vt_ref, qt_ref, *, tq, tk):
    qi = pl.program_id(2)
    n_kv = v_ref.shape[0] // tk
    cpq = tq // tk

    @pl.when(qi == 0)
    def _():
        for c in range(n_kv):
            vt_ref[c] = v_ref[c * tk:(c + 1) * tk, :].T

    qt_ref[...] = q_ref[...].T

    def scores(c, q0):
        return _dot(k_ref[c * tk:(c + 1) * tk, :], qt_ref[:, q0:])

    def update(c, st, state, q0, masked):
        m, l, acc = (x[:, q0:] for x in state)
        if masked:
            key = lax.broadcasted_iota(jnp.int32, st.shape, 0)
            qry = lax.broadcasted_iota(jnp.int32, st.shape, 1)
            st = jnp.where(key <= qry, st, -1e30)
        m_new = jnp.maximum(m, jnp.max(st, axis=0, keepdims=True))
        alpha = jnp.exp2(m - m_new)
        pt = jnp.exp2(st - m_new)
        l = alpha * l + jnp.sum(pt, axis=0, keepdims=True)
        acc = alpha * acc + _dot(vt_ref[c], pt.astype(BF16))
        if q0 == 0:
            return m_new, l, acc
        return tuple(jnp.concatenate([old[:, :q0], new], axis=1) for old, new in zip(state, (m_new, l, acc)))

    def q_tile_program(t):
        n_chunks = (t + 1) * cpq
        q0_of = lambda c: max(c - t * cpq, 0) * tk
        sts = {c: scores(c, q0_of(c)) for c in range(min(ATTN_LOOKAHEAD, n_chunks))}
        state = (jnp.full((1, tq), -1e30, F32), jnp.zeros((1, tq), F32), jnp.zeros((V_HEAD, tq), F32))
        for c in range(n_chunks):
            state = update(c, sts.pop(c), state, q0_of(c), c >= t * cpq)
            if c + ATTN_LOOKAHEAD < n_chunks:
                sts[c + ATTN_LOOKAHEAD] = scores(c + ATTN_LOOKAHEAD, q0_of(c + ATTN_LOOKAHEAD))
        _, l, acc = state
        out = (acc * (1.0 / l)).T
        o_ref[...] = (out * g_ref[...].astype(F32)).astype(BF16)

    for t in range(n_kv // cpq):
        pl.when(qi == t)(functools.partial(q_tile_program, t))


def _attention(q, k, v, gates, n_heads, tq, tk):
    b, s, _ = q.shape
    return pl.pallas_call(
        functools.partial(_attn_body, tq=tq, tk=tk),
        grid=(b, n_heads, s // tq),
        scratch_shapes=[pltpu.VMEM((s // tk, V_HEAD, tk), BF16),
                        pltpu.VMEM((HEAD_PAD, tq), BF16)],
        in_specs=[pl.BlockSpec((None, tq, HEAD_PAD), lambda bi, h, i: (bi, i, h)),
                  pl.BlockSpec((None, s, HEAD_PAD), lambda bi, h, i: (bi, 0, h)),
                  pl.BlockSpec((None, s, V_HEAD), lambda bi, h, i: (bi, 0, h)),
                  pl.BlockSpec((None, tq, V_HEAD), lambda bi, h, i: (bi, i, h))],
        out_specs=pl.BlockSpec((None, tq, V_HEAD), lambda bi, h, i: (bi, i, h)),
        out_shape=jax.ShapeDtypeStruct((b, s, n_heads * V_HEAD), BF16),
        compiler_params=_params("parallel", "parallel", "arbitrary"),
        name="mla_attention",
    )(q, k, v, gates)


def _conv_finish_body(y_ref, gcn_ref, bcn_ref, gate_ref, o_ref, *, ts, rf):
    nc = y_ref.shape[0]
    inv_c = 1.0 / (nc * LANES)
    for r0 in range(0, ts, rf):
        tot = y_ref[0, r0:r0 + rf, :]
        for c in range(1, nc):
            tot = tot + y_ref[c, r0:r0 + rf, :]
        mu = jnp.broadcast_to(jnp.sum(tot, axis=-1, keepdims=True) * inv_c, (rf, LANES))
        sq = None
        for c in range(nc):
            d = y_ref[c, r0:r0 + rf, :] - mu
            sq = d * d if sq is None else sq + d * d
        rstd = jnp.broadcast_to(lax.rsqrt(jnp.sum(sq, axis=-1, keepdims=True) * inv_c + EPS), (rf, LANES))
        for c in range(nc):
            z = (y_ref[c, r0:r0 + rf, :] - mu) * rstd * gcn_ref[c] + bcn_ref[c]
            z = z * jax.nn.sigmoid(z)
            g = gate_ref[r0:r0 + rf, c * LANES:(c + 1) * LANES].astype(F32)
            o_ref[r0:r0 + rf, c * LANES:(c + 1) * LANES] = (z * g).astype(BF16)


def _conv_finish(y, gcn_slab, bcn_slab, gates, gate_col0, layer, ts):
    b, nc, s, _ = y.shape
    c = nc * LANES
    gblk = gate_col0 // c
    return pl.pallas_call(
        functools.partial(_conv_finish_body, ts=ts, rf=min(64, ts)),
        grid=(b, s // ts),
        in_specs=[pl.BlockSpec((None, nc, ts, LANES), lambda bi, i: (bi, 0, i, 0)),
                  pl.BlockSpec((None, nc, 1, LANES), lambda bi, i: (layer, 0, 0, 0)),
                  pl.BlockSpec((None, nc, 1, LANES), lambda bi, i: (layer, 0, 0, 0)),
                  pl.BlockSpec((None, ts, c), lambda bi, i: (bi, i, gblk))],
        out_specs=pl.BlockSpec((None, ts, c), lambda bi, i: (bi, i, 0)),
        out_shape=jax.ShapeDtypeStruct((b, s, c), BF16),
        compiler_params=_params("parallel", "parallel"),
        name="conv_finish",
    )(y, gcn_slab, bcn_slab, gates)


def _merge_body(a_ref, c_ref, wo_ref, wpw_ref, sa_ref, sc_ref, o_ref):
    ya = _dot(a_ref[...], wo_ref[...])
    yc = _dot(c_ref[...], wpw_ref[...])
    o_ref[...] = (sa_ref[...].astype(F32) * ya + sc_ref[...].astype(F32) * yc).astype(BF16)


def _merge(a, cc, w_o, w_pw, layer, gates, col_a, col_c, tm, tn):
    t, mw = a.shape
    cw = cc.shape[1]
    d = w_o.shape[2]
    ja, jc = col_a // tn, col_c // tn
    return pl.pallas_call(
        _merge_body,
        grid=(t // tm, d // tn),
        in_specs=[pl.BlockSpec((tm, mw), lambda i, j: (i, 0)),
                  pl.BlockSpec((tm, cw), lambda i, j: (i, 0)),
                  pl.BlockSpec((None, mw, tn), lambda i, j: (layer, 0, j)),
                  pl.BlockSpec((None, cw, tn), lambda i, j: (layer, 0, j)),
                  pl.BlockSpec((tm, tn), lambda i, j: (i, ja + j)),
                  pl.BlockSpec((tm, tn), lambda i, j: (i, jc + j))],
        out_specs=pl.BlockSpec((tm, tn), lambda i, j: (i, j)),
        out_shape=jax.ShapeDtypeStruct((t, d), BF16),
        compiler_params=_params("parallel", "arbitrary"),
        name="merge_proj",
    )(a, cc, w_o, w_pw, gates, gates)


def _outproj_body(y_ref, w_ref, x_ref, o_ref):
    o_ref[...] = x_ref[...] + _dot(y_ref[...], w_ref[...])


def _outproj(y, w_out, layer, x, tm, tn):
    t, d = y.shape
    n = w_out.shape[2]
    return pl.pallas_call(
        _outproj_body,
        grid=(t // tm, n // tn),
        in_specs=[pl.BlockSpec((tm, d), lambda i, j: (i, 0)),
                  pl.BlockSpec((None, d, tn), lambda i, j: (layer, 0, j)),
                  pl.BlockSpec((tm, tn), lambda i, j: (i, j))],
        out_specs=pl.BlockSpec((tm, tn), lambda i, j: (i, j)),
        out_shape=jax.ShapeDtypeStruct((t, n), F32),
        compiler_params=_params("parallel", "arbitrary"),
        name="out_proj",
    )(y, w_out, x)


def _repack_body(w_ref, o_ref):
    o_ref[...] = w_ref[0].T.astype(BF16)


def _repack_tail(wt, row0, tr):
    depth, n, d = wt.shape
    return pl.pallas_call(
        _repack_body,
        grid=(depth, (n - row0) // tr),
        in_specs=[pl.BlockSpec((pl.Element(1), pl.Element(tr), pl.Element(d)),
                               lambda l, i: (l, pl.multiple_of(row0 + i * tr, math.gcd(row0, tr)), 0))],
        out_specs=pl.BlockSpec((None, d, tr), lambda l, i: (l, 0, i)),
        out_shape=jax.ShapeDtypeStruct((depth, d, n - row0), BF16),
        compiler_params=_params("parallel", "parallel"),
        name="w_in_repack",
    )(wt)


def _slab(p):
    l, r, c = p.shape
    return p.reshape(l, r, c // LANES, LANES).transpose(0, 2, 1, 3)


def _forward(x, positions, g_pre, w_in, g_q, w_q_up, g_kv, w_kv_up, w_o_mla, w_dw, b_dw, g_cn, b_cn,
             w_pw_out, w_out, g_final, *, n_heads):
    b, s, d = x.shape
    depth = g_pre.shape[0]
    t = b * s
    q_lora, kv_lora = g_q.shape[1], g_kv.shape[1]
    mw = n_heads * V_HEAD
    cw = w_dw.shape[2]
    off_kr = q_lora + kv_lora
    off_gmla = off_kr + QK_ROPE
    off_conv = off_gmla + mw
    off_gconv = off_conv + 2 * cw
    assert w_in.shape[2] == off_gconv + cw + 2 * d and cw == d

    tm = min(1024, s)
    tn = min(512, d)
    tn_gate = min(1024, d)
    tn_glu = min(512, d // 4)
    tq = min(2048, s)
    tk = min(256, s)
    ts = min(256, s)
    heads_per_tile = min(8, n_heads)

    swap = jnp.concatenate([jnp.arange(QK_ROPE // 2, QK_ROPE), jnp.arange(0, QK_ROPE // 2)])
    w_kr = w_in[:, :, off_kr:off_gmla]
    w_lat = jnp.concatenate([w_in[:, :, :off_gmla], w_kr[:, :, swap]], axis=2).astype(BF16)
    w_tail = _repack_tail(jnp.swapaxes(w_in, 1, 2), off_gmla, min(512, d))
    wq = w_q_up.reshape(depth, q_lora, n_heads, QK_NOPE + QK_ROPE)
    w_q = jnp.concatenate([wq, wq[..., QK_NOPE:][..., swap]], axis=3).reshape(depth, q_lora, n_heads * HEAD_PAD)
    w_q = w_q.astype(BF16)
    w_kv = w_kv_up.astype(BF16)
    w_o = w_o_mla.astype(BF16)
    w_pw = w_pw_out.astype(BF16)
    w_o2 = w_out.astype(BF16)
    w_slab = _slab(jnp.pad(w_dw, ((0, 0), (0, CONV_TAPS_PAD - CONV_K), (0, 0))))
    bdw_slab, gcn_slab, bcn_slab = _slab(b_dw[:, None]), _slab(g_cn[:, None]), _slab(b_cn[:, None])

    inv_freq = 1.0 / (ROPE_THETA ** (jnp.arange(0, QK_ROPE, 2, dtype=F32) / QK_ROPE))
    cs = _rope_table(positions.reshape(t, 1), jnp.tile(inv_freq, 4)[None], min(1024, t))

    q_scale = (1.0 / math.sqrt(QK_NOPE + QK_ROPE)) * math.log2(math.e)
    x2 = x.reshape(t, d)
    h = _rmsnorm(x2, g_pre[0][None], BF16, min(256, t))
    for l in range(depth):
        cq, ckv, kr = _latent(h, w_lat, l, g_q[l][None], g_kv[l][None], cs, min(512, s))
        g_mla = _gates(h, w_tail, l, 0, mw, mw, tm, tn_gate)
        g_mix = _gates(h, w_tail, l, mw + 2 * cw, cw + 2 * d, cw, tm, tn_gate)
        y_conv = _glu_conv(h, w_tail, l, mw, cw, w_slab, bdw_slab, s, tm, tn_glu)
        q = _qup(cq, w_q, l, cs, q_scale, tm, heads_per_tile)
        k, v = _kvup(ckv, w_kv, l, kr, tm, heads_per_tile)
        attn = _attention(q.reshape(b, s, -1), k.reshape(b, s, -1), v.reshape(b, s, -1), g_mla.reshape(b, s, mw),
                          n_heads, tq, tk)
        cc = _conv_finish(y_conv, gcn_slab, bcn_slab, g_mix.reshape(b, s, -1), 0, l, ts)
        y = _merge(attn.reshape(t, mw), cc.reshape(t, cw), w_o, w_pw, l, g_mix, cw, cw + d, tm, min(256, d))
        x2 = _outproj(y, w_o2, l, x2, tm, tn)
        if l + 1 < depth:
            h = _rmsnorm(x2, g_pre[l + 1][None], BF16, min(256, t))
    return _rmsnorm(x2, g_final[None], F32, min(256, t)).reshape(b, s, d)


def kernel(x, positions, g_pre, w_in, g_q, w_q_up, g_kv, w_kv_up, w_o_mla, w_dw, b_dw, g_cn, b_cn, w_pw_out, w_out, g_final):
    return _forward(x, positions, g_pre, w_in, g_q, w_q_up, g_kv, w_kv_up, w_o_mla, w_dw, b_dw, g_cn, b_cn,
                    w_pw_out, w_out, g_final, n_heads=N_HEADS)
```
